```python
import math
import jax, jax.numpy as jnp
from jax import lax
import numpy as np

D_MODEL = 4096
BATCH = 2
SEQ = 8192
DEPTH = 2

D_MIX = D_MODEL
D_FF = 2 * D_MODEL
NORM_EPS = 1e-6
A_HEAD_DIM = 128
A_WIDTH = D_MIX // 4
A_HEADS = A_WIDTH // A_HEAD_DIM
CONV_A_WIDTH = 3
DN_HEAD_DIM = 128
DN_WIDTH = D_MIX // 2
DN_HEADS = DN_WIDTH // DN_HEAD_DIM
DN_CONV_WIDTH = 4
DN_CHUNK = 64
POOL_WINDOWS = (2, 4, 8, 16)
POOL_GROUPS = 4
POOL_WIDTH = D_MIX - A_WIDTH - DN_WIDTH
POOL_GROUP_DIM = POOL_WIDTH // POOL_GROUPS
W_IN_COLS = 3 * A_WIDTH + POOL_WIDTH + 4 * DN_WIDTH + 2 * DN_HEADS

kernel_name = "hybrid_parallel_conv_pool_deltanet_macaron"


def rms_norm(x, gain):
    xf = x.astype(jnp.float32)
    xf = xf * lax.rsqrt(jnp.mean(xf * xf, axis=-1, keepdims=True) + NORM_EPS)
    return (xf * gain.astype(jnp.float32)).astype(x.dtype)


def l2_normalize(x):
    xf = x.astype(jnp.float32)
    return xf * lax.rsqrt(jnp.sum(xf * xf, axis=-1, keepdims=True) + NORM_EPS)


def swiglu(x, w_gate, w_up, w_down):
    return (jax.nn.silu(x @ w_gate) * (x @ w_up)) @ w_down


def causal_depthwise_conv(x, w):
    width, ch = w.shape
    return lax.conv_general_dilated(
        x, w[:, None, :], window_strides=(1,), padding=[(width - 1, 0)],
        dimension_numbers=("NWC", "WIO", "NWC"), feature_group_count=ch)


def multiscale_pool(u, pool_w, pool_scale):
    b, t, _ = u.shape
    uf = u.astype(jnp.float32).reshape(b, t, POOL_GROUPS, POOL_GROUP_DIM)
    cs = jnp.concatenate([jnp.zeros_like(uf[:, :1]), jnp.cumsum(uf, axis=1)], axis=1)
    pos = jnp.arange(1, t + 1, dtype=jnp.float32)
    outs = []
    for g, w in enumerate(POOL_WINDOWS):
        c = cs[:, :, g]
        lag = jnp.concatenate([jnp.zeros_like(c[:, :w - 1]), c[:, :t + 1 - w]], axis=1)
        count = jnp.minimum(pos, float(w))[None, :, None]
        outs.append((c[:, 1:] - lag) / count - uf[:, :, g])
    pooled = jnp.stack(outs, axis=2).astype(u.dtype)
    mixed = jnp.einsum("btgc,gcd->btgd", pooled, pool_w)
    return mixed.reshape(b, t, POOL_WIDTH) * pool_scale


def gated_delta_rule(q, k, v, g, beta):
    b, t, h, dk = q.shape
    dv = v.shape[-1]
    c = DN_CHUNK
    n = t // c

    def chunks(a):
        a = a.astype(jnp.float32).reshape((b, n, c, h) + a.shape[3:])
        return jnp.moveaxis(a, 3, 1)

    q = chunks(q) * (dk ** -0.5)
    k, v = chunks(k), chunks(v)
    g = jnp.cumsum(chunks(g), axis=-1)
    beta = chunks(beta)
    causal = jnp.tril(jnp.ones((c, c), dtype=bool))
    strict = jnp.tril(jnp.ones((c, c), dtype=bool), k=-1)
    decay = jnp.exp(jnp.where(causal, g[..., :, None] - g[..., None, :], -jnp.inf))
    k_beta = k * beta[..., None]
    kk = jnp.einsum("bhnid,bhnjd->bhnij", k_beta, k) * decay
    lower = jnp.where(strict, kk, 0.0) + jnp.eye(c, dtype=jnp.float32)
    rhs = jnp.concatenate([v * beta[..., None], k_beta * jnp.exp(g)[..., None]], axis=-1)
    sol = lax.linalg.triangular_solve(lower, rhs, left_side=True, lower=True, unit_diagonal=True)
    u, w = sol[..., :dv], sol[..., dv:]
    attn_qk = jnp.einsum("bhnid,bhnjd->bhnij", q, k) * decay
    g_last = g[..., -1]
    k_state = k * jnp.exp(g_last[..., None] - g)[..., None]
    q_state = q * jnp.exp(g)[..., None]

    def step(state, xs):
        q_s, k_s, u_c, w_c, a_qk, gl = xs
        v_new = u_c - jnp.einsum("bhck,bhkv->bhcv", w_c, state)
        o = jnp.einsum("bhck,bhkv->bhcv", q_s, state) + jnp.einsum("bhij,bhjv->bhiv", a_qk, v_new)
        state = state * jnp.exp(gl)[..., None, None] + jnp.einsum("bhck,bhcv->bhkv", k_s, v_new)
        return state, o

    xs = tuple(jnp.moveaxis(a, 2, 0) for a in (q_state, k_state, u, w, attn_qk, g_last))
    state0 = jnp.zeros((b, h, dk, dv), jnp.float32)
    _, o = lax.scan(step, state0, xs)
    return jnp.transpose(o, (1, 0, 3, 2, 4)).reshape(b, t, h, dv)


def token_mix(xn, w_in, conv_a, pool_w, pool_scale, conv_qkv, a_log, dt_bias, o_norm, w_out):
    b, t, _ = xn.shape
    f32 = jnp.float32
    proj = xn @ w_in
    c1 = A_WIDTH
    c2 = 2 * A_WIDTH
    c3 = 3 * A_WIDTH
    c4 = c3 + POOL_WIDTH
    c5 = c4 + 3 * DN_WIDTH
    c6 = c5 + DN_WIDTH
    c7 = c6 + DN_HEADS
    a_b, a_c, a_h, p_in, qkv, z, beta_raw, alpha_raw = jnp.split(proj, [c1, c2, c3, c4, c5, c6, c7], axis=-1)
    y_a = a_b * causal_depthwise_conv(a_c * a_h, conv_a)
    y_b = multiscale_pool(p_in, pool_w, pool_scale)
    qkv = jax.nn.silu(causal_depthwise_conv(qkv, conv_qkv))
    q, k, v = (a.reshape(b, t, DN_HEADS, DN_HEAD_DIM) for a in jnp.split(qkv, 3, axis=-1))
    q, k = l2_normalize(q), l2_normalize(k)
    beta = jax.nn.sigmoid(beta_raw.astype(f32))
    g = -jnp.exp(a_log.astype(f32)) * jax.nn.softplus(alpha_raw.astype(f32) + dt_bias.astype(f32))
    o = gated_delta_rule(q, k, v, g, beta)
    o = rms_norm(o, o_norm) * jax.nn.silu(z.reshape(b, t, DN_HEADS, DN_HEAD_DIM).astype(f32))
    y_c = o.reshape(b, t, DN_WIDTH).astype(xn.dtype)
    y = jnp.concatenate([y_a, y_b.astype(xn.dtype), y_c], axis=-1)
    return y @ w_out


def setup_inputs(seed: int = 0) -> dict:
    key = jax.random.key(seed)
    ks = jax.random.split(key, 24)
    f32 = jnp.float32

    def normal(k, shape, scale):
        return jax.random.normal(k, shape, f32) * scale

    def gain(k, shape):
        return 1.0 + 0.02 * jax.random.normal(k, shape, f32)

    dt = jnp.exp(jax.random.uniform(ks[12], (DEPTH, DN_HEADS), f32, math.log(1e-3), math.log(1e-1)))
    return {
        "x": normal(ks[0], (BATCH, SEQ, D_MODEL), 1.0),
        "ffn1_norm": gain(ks[1], (DEPTH, D_MODEL)),
        "ffn1_w_gate": normal(ks[2], (DEPTH, D_MODEL, D_FF), D_MODEL ** -0.5),
        "ffn1_w_up": normal(ks[3], (DEPTH, D_MODEL, D_FF), D_MODEL ** -0.5),
        "ffn1_w_down": normal(ks[4], (DEPTH, D_FF, D_MODEL), D_FF ** -0.5),
        "mix_norm": gain(ks[5], (DEPTH, D_MODEL)),
        "w_in": normal(ks[6], (DEPTH, D_MODEL, W_IN_COLS), D_MODEL ** -0.5),
        "conv_a": normal(ks[7], (DEPTH, CONV_A_WIDTH, A_WIDTH), CONV_A_WIDTH ** -0.5),
        "pool_w": normal(ks[8], (DEPTH, POOL_GROUPS, POOL_GROUP_DIM, POOL_GROUP_DIM), POOL_GROUP_DIM ** -0.5),
        "pool_scale": 1.0 + 0.1 * jax.random.normal(ks[9], (DEPTH, POOL_WIDTH), f32),
        "conv_qkv": normal(ks[10], (DEPTH, DN_CONV_WIDTH, 3 * DN_WIDTH), DN_CONV_WIDTH ** -0.5),
        "a_log": jnp.log(jax.random.uniform(ks[11], (DEPTH, DN_HEADS), f32, 1.0, 16.0)),
        "dt_bias": dt + jnp.log(-jnp.expm1(-dt)),
        "o_norm": gain(ks[13], (DEPTH, DN_HEAD_DIM)),
        "w_out": normal(ks[14], (DEPTH, D_MIX, D_MODEL), D_MIX ** -0.5),
        "ffn2_norm": gain(ks[15], (DEPTH, D_MODEL)),
        "ffn2_w_gate": normal(ks[16], (DEPTH, D_MODEL, D_FF), D_MODEL ** -0.5),
        "ffn2_w_up": normal(ks[17], (DEPTH, D_MODEL, D_FF), D_MODEL ** -0.5),
        "ffn2_w_down": normal(ks[18], (DEPTH, D_FF, D_MODEL), D_FF ** -0.5),
        "final_norm": gain(ks[19], (D_MODEL,)),
    }


def reference(x, ffn1_norm, ffn1_w_gate, ffn1_w_up, ffn1_w_down, mix_norm, w_in, conv_a, pool_w,
              pool_scale, conv_qkv, a_log, dt_bias, o_norm, w_out, ffn2_norm, ffn2_w_gate, ffn2_w_up,
              ffn2_w_down, final_norm):
    for layer in range(DEPTH):
        x = x + 0.5 * swiglu(rms_norm(x, ffn1_norm[layer]), ffn1_w_gate[layer], ffn1_w_up[layer], ffn1_w_down[layer])
        x = x + token_mix(rms_norm(x, mix_norm[layer]), w_in[layer], conv_a[layer], pool_w[layer],
                          pool_scale[layer], conv_qkv[layer], a_log[layer], dt_bias[layer],
                          o_norm[layer], w_out[layer])
        x = x + 0.5 * swiglu(rms_norm(x, ffn2_norm[layer]), ffn2_w_gate[layer], ffn2_w_up[layer], ffn2_w_down[layer])
    return rms_norm(x, final_norm)
```

```python
import functools

import jax
import jax.numpy as jnp
from jax import lax
from jax.experimental import pallas as pl
from jax.experimental.pallas import tpu as pltpu

F32 = jnp.float32
BF16 = jnp.bfloat16

NORM_EPS = 1e-6
A_WIDTH = 1024
POOL_WIDTH = 1024
POOL_WINDOWS = (2, 4, 8, 16)
POOL_GROUP_DIM = 256
DN_HEADS = 16
DN_HEAD_DIM = 128
DN_WIDTH = DN_HEADS * DN_HEAD_DIM
CHUNK = 64
HALO = 16
LANES = 128
VMEM_LIMIT = 56 * 1024 * 1024


def _params(*semantics):
    return pltpu.CompilerParams(dimension_semantics=semantics, vmem_limit_bytes=VMEM_LIMIT)


def _sigmoid(x):
    return 1.0 / (1.0 + jnp.exp(-x))


def _silu(x):
    return x * _sigmoid(x)


def _rmsnorm_body(x_ref, g_ref, o_ref):
    x = x_ref[...]
    ms = jnp.mean(x * x, axis=-1, keepdims=True)
    o_ref[...] = (x * lax.rsqrt(ms + NORM_EPS) * g_ref[...]).astype(o_ref.dtype)


def rmsnorm(x, gain, out_dtype, tm=512):
    m, d = x.shape
    return pl.pallas_call(
        _rmsnorm_body,
        grid=(m // tm,),
        in_specs=[pl.BlockSpec((tm, d), lambda i: (i, 0)), pl.BlockSpec((1, d), lambda i: (0, 0))],
        out_specs=pl.BlockSpec((tm, d), lambda i: (i, 0)),
        out_shape=jax.ShapeDtypeStruct((m, d), out_dtype),
        compiler_params=_params("parallel"),
        name="rmsnorm",
    )(x, gain.reshape(1, d))


def _mm_body(a_ref, b_ref, o_ref):
    o_ref[...] = jnp.dot(a_ref[...], b_ref[...], preferred_element_type=F32).astype(o_ref.dtype)


def matmul(a, b, out_dtype, tm, tn):
    m, k = a.shape
    n = b.shape[1]
    return pl.pallas_call(
        _mm_body,
        grid=(m // tm, n // tn),
        in_specs=[pl.BlockSpec((tm, k), lambda i, j: (i, 0)), pl.BlockSpec((k, tn), lambda i, j: (0, j))],
        out_specs=pl.BlockSpec((tm, tn), lambda i, j: (i, j)),
        out_shape=jax.ShapeDtypeStruct((m, n), out_dtype),
        compiler_params=_params("parallel", "arbitrary"),
        name="matmul",
    )(a, b)


def _gateup_body(a_ref, wg_ref, wu_ref, o_ref):
    a = a_ref[...]
    g = jnp.dot(a, wg_ref[...], preferred_element_type=F32)
    u = jnp.dot(a, wu_ref[...], preferred_element_type=F32)
    o_ref[...] = (_silu(g) * u).astype(o_ref.dtype)


def gate_up(a, wg, wu, tm=1024, tn=512):
    m, k = a.shape
    n = wg.shape[1]
    return pl.pallas_call(
        _gateup_body,
        grid=(m // tm, n // tn),
        in_specs=[pl.BlockSpec((tm, k), lambda i, j: (i, 0)),
                  pl.BlockSpec((k, tn), lambda i, j: (0, j)),
                  pl.BlockSpec((k, tn), lambda i, j: (0, j))],
        out_specs=pl.BlockSpec((tm, tn), lambda i, j: (i, j)),
        out_shape=jax.ShapeDtypeStruct((m, n), BF16),
        compiler_params=_params("parallel", "arbitrary"),
        name="gate_up",
    )(a, wg, wu)


def _down_body(h_ref, w_ref, x_ref, o_ref, acc_ref, *, scale):
    k = pl.program_id(2)

    @pl.when(k == 0)
    def _():
        acc_ref[...] = jnp.zeros_like(acc_ref)

    acc_ref[...] += jnp.dot(h_ref[...], w_ref[...], preferred_element_type=F32)

    @pl.when(k == pl.num_programs(2) - 1)
    def _():
        o_ref[...] = x_ref[...] + scale * acc_ref[...]


def down_residual(h, w, x, scale, tm=1024, tn=1024, tk=2048):
    m, k = h.shape
    n = w.shape[1]
    return pl.pallas_call(
        functools.partial(_down_body, scale=scale),
        grid=(m // tm, n // tn, k // tk),
        in_specs=[pl.BlockSpec((tm, tk), lambda i, j, kk: (i, kk)),
                  pl.BlockSpec((tk, tn), lambda i, j, kk: (kk, j)),
                  pl.BlockSpec((tm, tn), lambda i, j, kk: (i, j))],
        out_specs=pl.BlockSpec((tm, tn), lambda i, j, kk: (i, j)),
        out_shape=jax.ShapeDtypeStruct((m, n), F32),
        scratch_shapes=[pltpu.VMEM((tm, tn), F32)],
        compiler_params=_params("parallel", "parallel", "arbitrary"),
        name="down_residual",
    )(h, w, x)


def _wout_body(ya_ref, yc_ref, wa_ref, wc_ref, x_ref, o_ref):
    acc = jnp.dot(ya_ref[...], wa_ref[...], preferred_element_type=F32)
    acc = acc + jnp.dot(yc_ref[...], wc_ref[...], preferred_element_type=F32)
    o_ref[...] = x_ref[...] + acc


def wout_residual(y_ab, y_c, w_ab, w_c, x, tm=1024, tn=512):
    m, ka = y_ab.shape
    kc = y_c.shape[1]
    n = w_ab.shape[1]
    return pl.pallas_call(
        _wout_body,
        grid=(m // tm, n // tn),
        in_specs=[pl.BlockSpec((tm, ka), lambda i, j: (i, 0)),
                  pl.BlockSpec((tm, kc), lambda i, j: (i, 0)),
                  pl.BlockSpec((ka, tn), lambda i, j: (0, j)),
                  pl.BlockSpec((kc, tn), lambda i, j: (0, j)),
                  pl.BlockSpec((tm, tn), lambda i, j: (i, j))],
        out_specs=pl.BlockSpec((tm, tn), lambda i, j: (i, j)),
        out_shape=jax.ShapeDtypeStruct((m, n), F32),
        compiler_params=_params("parallel", "arbitrary"),
        name="wout_residual",
    )(y_ab, y_c, w_ab, w_c, x)


def _mix_ab_body(ab_ref, ac_ref, ah_ref, pu_ref, ach_ref, ahh_ref, puh_ref, cw_ref, pw_ref, ps_ref,
                 o_ref, hbuf, pbuf, *, tt, tiles_per_seq):
    i = pl.program_id(0) % tiles_per_seq
    first = i == 0
    hbuf[0:HALO, :] = jnp.where(first, 0.0, ach_ref[...] * ahh_ref[...])
    hbuf[HALO:, :] = ac_ref[...] * ah_ref[...]
    cw = cw_ref[...]
    width = cw.shape[0]
    conv = cw[width - 1:width, :] * hbuf[HALO:HALO + tt, :]
    for j in range(width - 1):
        off = HALO - (width - 1) + j
        conv = conv + cw[j:j + 1, :] * hbuf[off:off + tt, :]
    o_ref[:, 0:A_WIDTH] = (ab_ref[...] * conv).astype(o_ref.dtype)

    pbuf[0:HALO, :] = jnp.where(first, 0.0, puh_ref[...])
    pbuf[HALO:, :] = pu_ref[...]
    pos = (i * tt + lax.broadcasted_iota(jnp.int32, (tt, 1), 0) + 1).astype(F32)
    for g, w in enumerate(POOL_WINDOWS):
        lo = g * POOL_GROUP_DIM
        hi = lo + POOL_GROUP_DIM
        cur = pbuf[HALO:HALO + tt, lo:hi]
        s = cur
        for j in range(1, w):
            s = s + pbuf[HALO - j:HALO - j + tt, lo:hi]
        pooled = s / jnp.minimum(pos, float(w)) - cur
        mixed = jnp.dot(pooled.astype(BF16), pw_ref[g], preferred_element_type=F32)
        o_ref[:, A_WIDTH + lo:A_WIDTH + hi] = (mixed * ps_ref[:, lo:hi]).astype(o_ref.dtype)


def mix_ab(proj, conv_a, pool_w, pool_scale, seq, tt=512):
    m = proj.shape[0]
    cb = A_WIDTH
    hb = tt // HALO

    def main(col):
        return pl.BlockSpec((tt, cb), lambda p: (p, col))

    def halo(col):
        return pl.BlockSpec((HALO, cb), lambda p: (jnp.maximum(p * hb - 1, 0), col))

    return pl.pallas_call(
        functools.partial(_mix_ab_body, tt=tt, tiles_per_seq=seq // tt),
        grid=(m // tt,),
        in_specs=[main(0), main(1), main(2), main(3), halo(1), halo(2), halo(3),
                  pl.BlockSpec(conv_a.shape, lambda p: (0, 0)),
                  pl.BlockSpec(pool_w.shape, lambda p: (0, 0, 0)),
                  pl.BlockSpec((1, POOL_WIDTH), lambda p: (0, 0))],
        out_specs=pl.BlockSpec((tt, A_WIDTH + POOL_WIDTH), lambda p: (p, 0)),
        out_shape=jax.ShapeDtypeStruct((m, A_WIDTH + POOL_WIDTH), BF16),
        scratch_shapes=[pltpu.VMEM((tt + HALO, cb), F32), pltpu.VMEM((tt + HALO, cb), F32)],
        compiler_params=_params("parallel"),
        name="mix_ab",
    )(proj, proj, proj, proj, proj, proj, proj, conv_a, pool_w.astype(BF16), pool_scale.reshape(1, POOL_WIDTH))


def _qkv_body(x_ref, xh_ref, cw_ref, o_ref, buf, *, tt, tiles_per_seq):
    first = (pl.program_id(0) % tiles_per_seq) == 0
    normalise = pl.program_id(1) < 2
    buf[0:HALO, :] = jnp.where(first, 0.0, xh_ref[...])
    buf[HALO:, :] = x_ref[...]
    cw = cw_ref[...]
    width = cw.shape[0]
    for h in range(DN_HEADS):
        lo = h * DN_HEAD_DIM
        hi = lo + DN_HEAD_DIM
        c = cw[width - 1:width, lo:hi] * buf[HALO:HALO + tt, lo:hi]
        for j in range(width - 1):
            off = HALO - (width - 1) + j
            c = c + cw[j:j + 1, lo:hi] * buf[off:off + tt, lo:hi]
        a = _silu(c)
        inv = lax.rsqrt(jnp.sum(a * a, axis=-1, keepdims=True) + NORM_EPS)
        o_ref[:, lo:hi] = a * jnp.where(normalise, inv, 1.0)


def qkv_frontend(proj, conv_qkv, seq, col0, tt=512):
    m = proj.shape[0]
    cb0 = col0 // DN_WIDTH
    hb = tt // HALO
    return pl.pallas_call(
        functools.partial(_qkv_body, tt=tt, tiles_per_seq=seq // tt),
        grid=(m // tt, 3),
        in_specs=[pl.BlockSpec((tt, DN_WIDTH), lambda p, j: (p, cb0 + j)),
                  pl.BlockSpec((HALO, DN_WIDTH), lambda p, j: (jnp.maximum(p * hb - 1, 0), cb0 + j)),
                  pl.BlockSpec((conv_qkv.shape[0], DN_WIDTH), lambda p, j: (0, j))],
        out_specs=pl.BlockSpec((tt, DN_WIDTH), lambda p, j: (p, j)),
        out_shape=jax.ShapeDtypeStruct((m, 3 * DN_WIDTH), F32),
        scratch_shapes=[pltpu.VMEM((tt + HALO, DN_WIDTH), F32)],
        compiler_params=_params("parallel", "arbitrary"),
        name="qkv_frontend",
    )(proj, proj, conv_qkv)


def _split3(x):
    hi = x.astype(BF16)
    r = x - hi.astype(F32)
    mid = r.astype(BF16)
    lo = (r - mid.astype(F32)).astype(BF16)
    return hi, mid, lo


def _gates_body(x_ref, a_ref, dt_ref, o_ref, *, tt):
    x = x_ref[...]
    beta = _sigmoid(x)
    y = x + dt_ref[...]
    softplus = jnp.maximum(y, 0.0) + jnp.log(1.0 + jnp.exp(-jnp.abs(y)))
    g = -jnp.exp(a_ref[...]) * softplus
    row = lax.broadcasted_iota(jnp.int32, (CHUNK, CHUNK), 0)
    col = lax.broadcasted_iota(jnp.int32, (CHUNK, CHUNK), 1)
    tril = (row >= col).astype(BF16)
    lane = lax.broadcasted_iota(jnp.int32, (CHUNK, LANES), 1)
    for c in range(tt // CHUNK):
        sl = slice(c * CHUNK, (c + 1) * CHUNK)
        gc = sum(jnp.dot(tril, p, preferred_element_type=F32) for p in _split3(g[sl, :]))
        o_ref[sl, :] = jnp.where(lane < DN_HEADS, beta[sl, :], gc)


def gates(pg, a_log, dt_bias, tt=512):
    m = pg.shape[0]
    pad = jnp.zeros((1, LANES), F32)
    a_row = pad.at[0, DN_HEADS:2 * DN_HEADS].set(a_log)
    dt_row = pad.at[0, DN_HEADS:2 * DN_HEADS].set(dt_bias)
    return pl.pallas_call(
        functools.partial(_gates_body, tt=tt),
        grid=(m // tt,),
        in_specs=[pl.BlockSpec((tt, LANES), lambda p: (p, 0)),
                  pl.BlockSpec((1, LANES), lambda p: (0, 0)),
                  pl.BlockSpec((1, LANES), lambda p: (0, 0))],
        out_specs=pl.BlockSpec((tt, LANES), lambda p: (p, 0)),
        out_shape=jax.ShapeDtypeStruct((m, LANES), F32),
        compiler_params=_params("parallel"),
        name="gates",
    )(pg, a_row, dt_row)


def _dot(a, b):
    return jnp.dot(a.astype(BF16), b.astype(BF16), preferred_element_type=F32)


def _dot_nt(a, b):
    return lax.dot_general(a.astype(BF16), b.astype(BF16), (((1,), (1,)), ((), ())), preferred_element_type=F32)


def _dot_tn(a, b):
    return lax.dot_general(a.astype(BF16), b.astype(BF16), (((0,), (0,)), ((), ())), preferred_element_type=F32)


def _unit_lower_inverse(strict_lower, row, col):
    def level_mask(s):
        return ((row // (2 * s)) == (col // (2 * s))) & ((row // s) % 2 == 1) & ((col // s) % 2 == 0)

    eye = (row == col).astype(F32)
    x = eye - jnp.where(level_mask(1), strict_lower, 0.0)
    s = 2
    while s < CHUNK:
        c_s = jnp.where(level_mask(s), strict_lower, 0.0)
        x = x - _dot(x, _dot(c_s, x))
        s *= 2
    return x


def _delta_body(q_ref, k_ref, v_ref, z_ref, gb_ref, on_ref, o_ref, s_ref, *, tb):
    head = pl.program_id(0) % DN_HEADS

    @pl.when(pl.program_id(1) == 0)
    def _():
        s_ref[...] = jnp.zeros_like(s_ref)

    sel_r = lax.broadcasted_iota(jnp.int32, (LANES, LANES), 0)
    e_beta = (sel_r == head).astype(BF16)
    e_g = (sel_r == head + DN_HEADS).astype(BF16)
    parts = _split3(gb_ref[...])
    beta_all = sum(jnp.dot(p, e_beta, preferred_element_type=F32) for p in parts)
    gc_all = sum(jnp.dot(p, e_g, preferred_element_type=F32) for p in parts)

    row = lax.broadcasted_iota(jnp.int32, (CHUNK, CHUNK), 0)
    col = lax.broadcasted_iota(jnp.int32, (CHUNK, CHUNK), 1)
    causal = row >= col
    strict = row > col
    lane0 = (lax.broadcasted_iota(jnp.int32, (CHUNK, LANES), 1) == 0).astype(BF16)
    scale = DN_HEAD_DIM ** -0.5
    gain = on_ref[...]

    state = s_ref[...]
    for c in range(tb // CHUNK):
        sl = slice(c * CHUNK, (c + 1) * CHUNK)
        q = q_ref[sl, :] * scale
        k = k_ref[sl, :]
        v = v_ref[sl, :]
        beta = beta_all[sl, :]
        gc = gc_all[sl, :]
        gc_row = sum(_dot_nt(lane0, p) for p in _split3(gc))
        decay = jnp.exp(jnp.where(causal, gc[:, 0:CHUNK] - gc_row, -jnp.inf))
        k_beta = k * beta
        kk = _dot_nt(k_beta, k) * decay
        attn = _dot_nt(q, k) * decay
        t_inv = _unit_lower_inverse(jnp.where(strict, kk, 0.0), row, col)
        e_gc = jnp.exp(gc)
        rhs = jnp.concatenate([v * beta, k_beta * e_gc], axis=1)
        sol = _dot(t_inv, rhs)
        u = sol[:, 0:DN_HEAD_DIM]
        w = sol[:, DN_HEAD_DIM:]
        g_last = gc[CHUNK - 1:CHUNK, :]
        k_state = k * jnp.exp(g_last - gc)
        q_state = q * e_gc

        v_new = u - _dot(w, state)
        o = _dot(q_state, state) + _dot(attn, v_new)
        state = state * jnp.exp(g_last) + _dot_tn(k_state, v_new)

        ms = jnp.mean(o * o, axis=-1, keepdims=True)
        z = z_ref[sl, :]
        o_ref[sl, :] = (o * lax.rsqrt(ms + NORM_EPS) * gain * _silu(z)).astype(o_ref.dtype)
    s_ref[...] = state


def delta_rule(qkv, proj, gb, o_norm, seq, z_col0, tb=512):
    m = qkv.shape[0]
    nb = m // seq
    tps = seq // tb
    zb = z_col0 // DN_HEAD_DIM

    def rows(bh, t):
        return (bh // DN_HEADS) * tps + t

    def head_block(col_block0):
        return pl.BlockSpec((tb, DN_HEAD_DIM), lambda bh, t: (rows(bh, t), col_block0 + bh % DN_HEADS))

    return pl.pallas_call(
        functools.partial(_delta_body, tb=tb),
        grid=(nb * DN_HEADS, tps),
        in_specs=[head_block(0), head_block(DN_HEADS), head_block(2 * DN_HEADS), head_block(zb),
                  pl.BlockSpec((tb, LANES), lambda bh, t: (rows(bh, t), 0)),
                  pl.BlockSpec((1, DN_HEAD_DIM), lambda bh, t: (0, 0))],
        out_specs=head_block(0),
        out_shape=jax.ShapeDtypeStruct((m, DN_WIDTH), BF16),
        scratch_shapes=[pltpu.VMEM((DN_HEAD_DIM, DN_HEAD_DIM), F32)],
        compiler_params=_params("parallel", "arbitrary"),
        name="delta_rule",
    )(qkv, qkv, qkv, proj, gb, o_norm.reshape(1, DN_HEAD_DIM))


def _ffn(x, norm, w_gate, w_up, w_down):
    xn = rmsnorm(x, norm, BF16)
    h = gate_up(xn, w_gate.astype(BF16), w_up.astype(BF16))
    return down_residual(h, w_down.astype(BF16), x, 0.5)


def _token_mix(x, seq, norm, w_in, conv_a, pool_w, pool_scale, conv_qkv, a_log, dt_bias, o_norm, w_out):
    xn = rmsnorm(x, norm, BF16)
    n_main = 3 * A_WIDTH + POOL_WIDTH + 4 * DN_WIDTH
    w_main = w_in[:, :n_main].astype(BF16)
    w_gates = jnp.pad(w_in[:, n_main:], ((0, 0), (0, LANES - 2 * DN_HEADS))).astype(BF16)
    proj = matmul(xn, w_main, F32, 1024, 1024)
    pg = matmul(xn, w_gates, F32, 1024, LANES)
    y_ab = mix_ab(proj, conv_a, pool_w, pool_scale, seq)
    qkv = qkv_frontend(proj, conv_qkv, seq, 3 * A_WIDTH + POOL_WIDTH)
    gb = gates(pg, a_log, dt_bias)
    y_c = delta_rule(qkv, proj, gb, o_norm, seq, 3 * A_WIDTH + POOL_WIDTH + 3 * DN_WIDTH)
    w_out16 = w_out.astype(BF16)
    k_ab = A_WIDTH + POOL_WIDTH
    return wout_residual(y_ab, y_c, w_out16[:k_ab], w_out16[k_ab:], x)


@jax.jit
def kernel(x, ffn1_norm, ffn1_w_gate, ffn1_w_up, ffn1_w_down, mix_norm, w_in, conv_a, pool_w, pool_scale,
           conv_qkv, a_log, dt_bias, o_norm, w_out, ffn2_norm, ffn2_w_gate, ffn2_w_up, ffn2_w_down, final_norm):
    b, t, d = x.shape
    h = x.reshape(b * t, d)
    for l in range(ffn1_norm.shape[0]):
        h = _ffn(h, ffn1_norm[l], ffn1_w_gate[l], ffn1_w_up[l], ffn1_w_down[l])
        h = _token_mix(h, t, mix_norm[l], w_in[l], conv_a[l], pool_w[l], pool_scale[l], conv_qkv[l],
                       a_log[l], dt_bias[l], o_norm[l], w_out[l])
        h = _ffn(h, ffn2_norm[l], ffn2_w_gate[l], ffn2_w_up[l], ffn2_w_down[l])
    return rmsnorm(h, final_norm, F32).reshape(b, t, d)
```

```python
import functools

import jax
import jax.numpy as jnp
from jax import lax
from jax.experimental import pallas as pl
from jax.experimental.pallas import tpu as pltpu

F32 = jnp.float32
BF16 = jnp.bfloat16

NORM_EPS = 1e-6
A_WIDTH = 1024
POOL_WIDTH = 1024
POOL_WINDOWS = (2, 4, 8, 16)
POOL_GROUP_DIM = 256
DN_HEADS = 16
DN_HEAD_DIM = 128
DN_WIDTH = DN_HEADS * DN_HEAD_DIM
CHUNK = 64
HALO = 16
LANES = 128
SUBLANES = 8
VMEM_LIMIT = 56 * 1024 * 1024


def _params(*semantics):
    return pltpu.CompilerParams(dimension_semantics=semantics, vmem_limit_bytes=VMEM_LIMIT)


def _sigmoid(x):
    return 1.0 / (1.0 + jnp.exp(-x))


def _silu(x):
    return x * _sigmoid(x)


def _rmsnorm_body(x_ref, g_ref, o_ref):
    x = x_ref[...]
    ms = jnp.mean(x * x, axis=-1, keepdims=True)
    o_ref[...] = (x * lax.rsqrt(ms + NORM_EPS) * g_ref[...]).astype(o_ref.dtype)


def rmsnorm(x, gain, out_dtype, tm=512):
    m, d = x.shape
    return pl.pallas_call(
        _rmsnorm_body,
        grid=(m // tm,),
        in_specs=[pl.BlockSpec((tm, d), lambda i: (i, 0)), pl.BlockSpec((1, d), lambda i: (0, 0))],
        out_specs=pl.BlockSpec((tm, d), lambda i: (i, 0)),
        out_shape=jax.ShapeDtypeStruct((m, d), out_dtype),
        compiler_params=_params("parallel"),
        name="rmsnorm",
    )(x, gain.reshape(1, d))


def _mm_body(a_ref, b_ref, o_ref):
    b = b_ref[...].astype(BF16)
    o_ref[...] = jnp.dot(a_ref[...], b, preferred_element_type=F32).astype(o_ref.dtype)


def matmul(a, b, out_dtype, tm, tn):
    m, k = a.shape
    n = b.shape[1]
    return pl.pallas_call(
        _mm_body,
        grid=(m // tm, n // tn),
        in_specs=[pl.BlockSpec((tm, k), lambda i, j: (i, 0)), pl.BlockSpec((k, tn), lambda i, j: (0, j))],
        out_specs=pl.BlockSpec((tm, tn), lambda i, j: (i, j)),
        out_shape=jax.ShapeDtypeStruct((m, n), out_dtype),
        compiler_params=_params("parallel", "arbitrary"),
        name="matmul",
    )(a, b)


def matmul_layer(a, w, layer, n, out_dtype, tm, tn):
    m, k = a.shape
    return pl.pallas_call(
        _mm_body,
        grid=(m // tm, n // tn),
        in_specs=[pl.BlockSpec((tm, k), lambda i, j: (i, 0)),
                  pl.BlockSpec((None, k, tn), lambda i, j: (layer, 0, j))],
        out_specs=pl.BlockSpec((tm, tn), lambda i, j: (i, j)),
        out_shape=jax.ShapeDtypeStruct((m, n), out_dtype),
        compiler_params=_params("parallel", "arbitrary"),
        name="matmul_layer",
    )(a, w)


def _gateup_body(a_ref, wg_ref, wu_ref, o_ref):
    a = a_ref[...]
    g = jnp.dot(a, wg_ref[...].astype(BF16), preferred_element_type=F32)
    u = jnp.dot(a, wu_ref[...].astype(BF16), preferred_element_type=F32)
    o_ref[...] = (_silu(g) * u).astype(o_ref.dtype)


def gate_up(a, wg, wu, layer, tm=1024, tn=256):
    m, k = a.shape
    n = wg.shape[2]
    return pl.pallas_call(
        _gateup_body,
        grid=(m // tm, n // tn),
        in_specs=[pl.BlockSpec((tm, k), lambda i, j: (i, 0)),
                  pl.BlockSpec((None, k, tn), lambda i, j: (layer, 0, j)),
                  pl.BlockSpec((None, k, tn), lambda i, j: (layer, 0, j))],
        out_specs=pl.BlockSpec((tm, tn), lambda i, j: (i, j)),
        out_shape=jax.ShapeDtypeStruct((m, n), BF16),
        compiler_params=_params("parallel", "arbitrary"),
        name="gate_up",
    )(a, wg, wu)


def _down_body(h_ref, w_ref, x_ref, o_ref, acc_ref, *, scale):
    k = pl.program_id(2)

    @pl.when(k == 0)
    def _():
        acc_ref[...] = jnp.zeros_like(acc_ref)

    acc_ref[...] += jnp.dot(h_ref[...], w_ref[...], preferred_element_type=F32)

    @pl.when(k == pl.num_programs(2) - 1)
    def _():
        o_ref[...] = x_ref[...] + scale * acc_ref[...]


def down_residual(h, w, x, scale, tm=1024, tn=1024, tk=2048):
    m, k = h.shape
    n = w.shape[1]
    return pl.pallas_call(
        functools.partial(_down_body, scale=scale),
        grid=(m // tm, n // tn, k // tk),
        in_specs=[pl.BlockSpec((tm, tk), lambda i, j, kk: (i, kk)),
                  pl.BlockSpec((tk, tn), lambda i, j, kk: (kk, j)),
                  pl.BlockSpec((tm, tn), lambda i, j, kk: (i, j))],
        out_specs=pl.BlockSpec((tm, tn), lambda i, j, kk: (i, j)),
        out_shape=jax.ShapeDtypeStruct((m, n), F32),
        scratch_shapes=[pltpu.VMEM((tm, tn), F32)],
        compiler_params=_params("parallel", "parallel", "arbitrary"),
        name="down_residual",
    )(h, w, x)


def _wout_body(ya_ref, yc_ref, wa_ref, wc_ref, x_ref, o_ref):
    acc = jnp.dot(ya_ref[...], wa_ref[...].astype(BF16), preferred_element_type=F32)
    acc = acc + jnp.dot(yc_ref[...], wc_ref[...].astype(BF16), preferred_element_type=F32)
    o_ref[...] = x_ref[...] + acc


def wout_residual(y_ab, y_c, w, layer, x, tm=1024, tn=512):
    m, ka = y_ab.shape
    kc = y_c.shape[1]
    n = w.shape[2]
    assert ka == kc, "the two K halves share one block shape"
    return pl.pallas_call(
        _wout_body,
        grid=(m // tm, n // tn),
        in_specs=[pl.BlockSpec((tm, ka), lambda i, j: (i, 0)),
                  pl.BlockSpec((tm, kc), lambda i, j: (i, 0)),
                  pl.BlockSpec((None, ka, tn), lambda i, j: (layer, 0, j)),
                  pl.BlockSpec((None, kc, tn), lambda i, j: (layer, 1, j)),
                  pl.BlockSpec((tm, tn), lambda i, j: (i, j))],
        out_specs=pl.BlockSpec((tm, tn), lambda i, j: (i, j)),
        out_shape=jax.ShapeDtypeStruct((m, n), F32),
        compiler_params=_params("parallel", "arbitrary"),
        name="wout_residual",
    )(y_ab, y_c, w, w, x)


def _mix_ab_body(ab_ref, ac_ref, ah_ref, pu_ref, ach_ref, ahh_ref, puh_ref, cw_ref, pw_ref, ps_ref,
                 o_ref, hbuf, pbuf, *, tt, tiles_per_seq):
    i = pl.program_id(0) % tiles_per_seq
    first = i == 0
    hbuf[0:HALO, :] = jnp.where(first, 0.0, ach_ref[...] * ahh_ref[...])
    hbuf[HALO:, :] = ac_ref[...] * ah_ref[...]
    cw = cw_ref[...]
    width = cw.shape[0]
    conv = cw[width - 1:width, :] * hbuf[HALO:HALO + tt, :]
    for j in range(width - 1):
        off = HALO - (width - 1) + j
        conv = conv + cw[j:j + 1, :] * hbuf[off:off + tt, :]
    o_ref[:, 0:A_WIDTH] = (ab_ref[...] * conv).astype(o_ref.dtype)

    pbuf[0:HALO, :] = jnp.where(first, 0.0, puh_ref[...])
    pbuf[HALO:, :] = pu_ref[...]
    pos = (i * tt + lax.broadcasted_iota(jnp.int32, (tt, 1), 0) + 1).astype(F32)
    for g, w in enumerate(POOL_WINDOWS):
        lo = g * POOL_GROUP_DIM
        hi = lo + POOL_GROUP_DIM
        cur = pbuf[HALO:HALO + tt, lo:hi]
        s = cur
        for j in range(1, w):
            s = s + pbuf[HALO - j:HALO - j + tt, lo:hi]
        pooled = s / jnp.minimum(pos, float(w)) - cur
        mixed = jnp.dot(pooled.astype(BF16), pw_ref[g], preferred_element_type=F32)
        o_ref[:, A_WIDTH + lo:A_WIDTH + hi] = (mixed * ps_ref[:, lo:hi]).astype(o_ref.dtype)


def mix_ab(proj, conv_a, pool_w, pool_scale, seq, tt=512):
    m = proj.shape[0]
    cb = A_WIDTH
    hb = tt // HALO

    def main(col):
        return pl.BlockSpec((tt, cb), lambda p: (p, col))

    def halo(col):
        return pl.BlockSpec((HALO, cb), lambda p: (jnp.maximum(p * hb - 1, 0), col))

    return pl.pallas_call(
        functools.partial(_mix_ab_body, tt=tt, tiles_per_seq=seq // tt),
        grid=(m // tt,),
        in_specs=[main(0), main(1), main(2), main(3), halo(1), halo(2), halo(3),
                  pl.BlockSpec(conv_a.shape, lambda p: (0, 0)),
                  pl.BlockSpec(pool_w.shape, lambda p: (0, 0, 0)),
                  pl.BlockSpec((1, POOL_WIDTH), lambda p: (0, 0))],
        out_specs=pl.BlockSpec((tt, A_WIDTH + POOL_WIDTH), lambda p: (p, 0)),
        out_shape=jax.ShapeDtypeStruct((m, A_WIDTH + POOL_WIDTH), BF16),
        scratch_shapes=[pltpu.VMEM((tt + HALO, cb), F32), pltpu.VMEM((tt + HALO, cb), F32)],
        compiler_params=_params("parallel"),
        name="mix_ab",
    )(proj, proj, proj, proj, proj, proj, proj, conv_a, pool_w.astype(BF16), pool_scale.reshape(1, POOL_WIDTH))


def _qkv_body(x_ref, xh_ref, cw_ref, o_ref, buf, *, tt, tiles_per_seq):
    first = (pl.program_id(0) % tiles_per_seq) == 0
    normalise = pl.program_id(1) < 2
    buf[0:HALO, :] = jnp.where(first, 0.0, xh_ref[...])
    buf[HALO:, :] = x_ref[...]
    cw = cw_ref[...]
    width = cw.shape[0]
    for h in range(DN_HEADS):
        lo = h * DN_HEAD_DIM
        hi = lo + DN_HEAD_DIM
        c = cw[width - 1:width, lo:hi] * buf[HALO:HALO + tt, lo:hi]
        for j in range(width - 1):
            off = HALO - (width - 1) + j
            c = c + cw[j:j + 1, lo:hi] * buf[off:off + tt, lo:hi]
        a = _silu(c)
        inv = lax.rsqrt(jnp.sum(a * a, axis=-1, keepdims=True) + NORM_EPS)
        o_ref[:, lo:hi] = a * jnp.where(normalise, inv, 1.0)


def qkv_frontend(proj, conv_qkv, seq, col0, tt=512):
    m = proj.shape[0]
    cb0 = col0 // DN_WIDTH
    hb = tt // HALO
    return pl.pallas_call(
        functools.partial(_qkv_body, tt=tt, tiles_per_seq=seq // tt),
        grid=(m // tt, 3),
        in_specs=[pl.BlockSpec((tt, DN_WIDTH), lambda p, j: (p, cb0 + j)),
                  pl.BlockSpec((HALO, DN_WIDTH), lambda p, j: (jnp.maximum(p * hb - 1, 0), cb0 + j)),
                  pl.BlockSpec((conv_qkv.shape[0], DN_WIDTH), lambda p, j: (0, j))],
        out_specs=pl.BlockSpec((tt, DN_WIDTH), lambda p, j: (p, j)),
        out_shape=jax.ShapeDtypeStruct((m, 3 * DN_WIDTH), F32),
        scratch_shapes=[pltpu.VMEM((tt + HALO, DN_WIDTH), F32)],
        compiler_params=_params("parallel", "arbitrary"),
        name="qkv_frontend",
    )(proj, proj, conv_qkv)


def _split3(x):
    hi = x.astype(BF16)
    r = x - hi.astype(F32)
    mid = r.astype(BF16)
    lo = (r - mid.astype(F32)).astype(BF16)
    return hi, mid, lo


def _gates_body(x_ref, a_ref, dt_ref, o_ref, *, tt):
    x = x_ref[...]
    beta = _sigmoid(x)
    y = x + dt_ref[...]
    softplus = jnp.maximum(y, 0.0) + jnp.log(1.0 + jnp.exp(-jnp.abs(y)))
    g = -jnp.exp(a_ref[...]) * softplus
    row = lax.broadcasted_iota(jnp.int32, (CHUNK, CHUNK), 0)
    col = lax.broadcasted_iota(jnp.int32, (CHUNK, CHUNK), 1)
    tril = (row >= col).astype(BF16)
    lane = lax.broadcasted_iota(jnp.int32, (CHUNK, LANES), 1)
    for c in range(tt // CHUNK):
        sl = slice(c * CHUNK, (c + 1) * CHUNK)
        gc = sum(jnp.dot(tril, p, preferred_element_type=F32) for p in _split3(g[sl, :]))
        o_ref[sl, :] = jnp.where(lane < DN_HEADS, beta[sl, :], gc)


def gates(pg, a_log, dt_bias, tt=512):
    m = pg.shape[0]
    pad = jnp.zeros((1, LANES), F32)
    a_row = pad.at[0, DN_HEADS:2 * DN_HEADS].set(a_log)
    dt_row = pad.at[0, DN_HEADS:2 * DN_HEADS].set(dt_bias)
    return pl.pallas_call(
        functools.partial(_gates_body, tt=tt),
        grid=(m // tt,),
        in_specs=[pl.BlockSpec((tt, LANES), lambda p: (p, 0)),
                  pl.BlockSpec((1, LANES), lambda p: (0, 0)),
                  pl.BlockSpec((1, LANES), lambda p: (0, 0))],
        out_specs=pl.BlockSpec((tt, LANES), lambda p: (p, 0)),
        out_shape=jax.ShapeDtypeStruct((m, LANES), F32),
        compiler_params=_params("parallel"),
        name="gates",
    )(pg, a_row, dt_row)


def _dot(a, b):
    return jnp.dot(a.astype(BF16), b.astype(BF16), preferred_element_type=F32)


def _dot_nt(a, b):
    return lax.dot_general(a.astype(BF16), b.astype(BF16), (((1,), (1,)), ((), ())), preferred_element_type=F32)


def _dot_tn(a, b):
    return lax.dot_general(a.astype(BF16), b.astype(BF16), (((0,), (0,)), ((), ())), preferred_element_type=F32)


def _level_mask(row, col, s):
    return ((row // (2 * s)) == (col // (2 * s))) & ((row // s) % 2 == 1) & ((col // s) % 2 == 0)


def _delta_body(q_ref, k_ref, v_ref, z_ref, gb_ref, gt_ref, on_ref, o_ref, s_ref, *, tb, hb):
    head0 = (pl.program_id(0) % (DN_HEADS // hb)) * hb
    nc = tb // CHUNK
    items = [(j, c) for j in range(hb) for c in range(nc)]

    @pl.when(pl.program_id(1) == 0)
    def _():
        s_ref[...] = jnp.zeros_like(s_ref)

    def rows(c):
        return slice(c * CHUNK, (c + 1) * CHUNK)

    def lanes(j):
        return slice(j * DN_HEAD_DIM, (j + 1) * DN_HEAD_DIM)

    sel_r = lax.broadcasted_iota(jnp.int32, (LANES, 2 * hb * LANES), 0)
    sel_c = lax.broadcasted_iota(jnp.int32, (LANES, 2 * hb * LANES), 1)
    src_lane = head0 + sel_c // (2 * LANES) + DN_HEADS * ((sel_c // LANES) % 2)
    select = (sel_r == src_lane).astype(BF16)
    bcast = sum(jnp.dot(p, select, preferred_element_type=F32) for p in _split3(gb_ref[...]))
    gt_row0 = (DN_HEADS + head0) % SUBLANES
    gt_rows = [gt_ref[pl.ds(gt_row0 + j, 1), :] for j in range(hb)]

    row = lax.broadcasted_iota(jnp.int32, (CHUNK, CHUNK), 0)
    col = lax.broadcasted_iota(jnp.int32, (CHUNK, CHUNK), 1)
    causal = row >= col
    strict = row > col
    eye = (row == col).astype(F32)
    scale = DN_HEAD_DIM ** -0.5
    gain = on_ref[...]

    q, k, kb, beta, gc, e_gc, g_last = {}, {}, {}, {}, {}, {}, {}
    decay, rhs = {}, {}
    for it in items:
        j, c = it
        q[it] = q_ref[rows(c), lanes(j)] * scale
        k[it] = k_ref[rows(c), lanes(j)]
        beta[it] = bcast[rows(c), 2 * j * LANES:(2 * j + 1) * LANES]
        gc[it] = bcast[rows(c), (2 * j + 1) * LANES:(2 * j + 2) * LANES]
        gc_row = gt_rows[j][:, c * CHUNK:(c + 1) * CHUNK]
        decay[it] = jnp.exp(jnp.where(causal, gc[it][:, 0:CHUNK] - gc_row, -jnp.inf))
        kb[it] = k[it] * beta[it]
        e_gc[it] = jnp.exp(gc[it])
        g_last[it] = gc[it][CHUNK - 1:CHUNK, :]
        rhs[it] = jnp.concatenate([v_ref[rows(c), lanes(j)] * beta[it], kb[it] * e_gc[it]], axis=1)

    kk = {it: _dot_nt(kb[it], k[it]) * decay[it] for it in items}
    attn = {it: _dot_nt(q[it], k[it]) * decay[it] for it in items}

    lower = {it: jnp.where(strict, kk[it], 0.0) for it in items}
    x = {it: eye - jnp.where(_level_mask(row, col, 1), lower[it], 0.0) for it in items}
    s = 2
    while s < CHUNK:
        mask = _level_mask(row, col, s)
        y = {it: _dot(jnp.where(mask, lower[it], 0.0), x[it]) for it in items}
        z = {it: _dot(x[it], y[it]) for it in items}
        x = {it: x[it] - z[it] for it in items}
        s *= 2

    sol = {it: _dot(x[it], rhs[it]) for it in items}
    k_state = {it: k[it] * jnp.exp(g_last[it] - gc[it]) for it in items}
    q_state = {it: q[it] * e_gc[it] for it in items}

    state = [s_ref[j] for j in range(hb)]
    for c in range(nc):
        its = [(j, c) for j in range(hb)]
        ws = [_dot(sol[it][:, DN_HEAD_DIM:], state[it[0]]) for it in its]
        qs = [_dot(q_state[it], state[it[0]]) for it in its]
        v_new = [sol[it][:, 0:DN_HEAD_DIM] - ws[it[0]] for it in its]
        upd = [_dot_tn(k_state[it], v_new[it[0]]) for it in its]
        intra = [_dot(attn[it], v_new[it[0]]) for it in its]
        for j in range(hb):
            state[j] = state[j] * jnp.exp(g_last[(j, c)]) + upd[j]
            o = qs[j] + intra[j]
            ms = jnp.mean(o * o, axis=-1, keepdims=True)
            zg = z_ref[rows(c), lanes(j)]
            o_ref[rows(c), lanes(j)] = (o * lax.rsqrt(ms + NORM_EPS) * gain * _silu(zg)).astype(o_ref.dtype)
    for j in range(hb):
        s_ref[j] = state[j]


def delta_rule(qkv, proj, gb, o_norm, seq, z_col0, tb=1024, hb=4):
    m = qkv.shape[0]
    tps = seq // tb
    groups = DN_HEADS // hb
    width = hb * DN_HEAD_DIM
    zb = z_col0 // width
    gb_t = gb.T

    def rows(bg, t):
        return (bg // groups) * tps + t

    def head_block(col_block0):
        return pl.BlockSpec((tb, width), lambda bg, t: (rows(bg, t), col_block0 + bg % groups))

    return pl.pallas_call(
        functools.partial(_delta_body, tb=tb, hb=hb),
        grid=(m // seq * groups, tps),
        in_specs=[head_block(0), head_block(groups), head_block(2 * groups), head_block(zb),
                  pl.BlockSpec((tb, LANES), lambda bg, t: (rows(bg, t), 0)),
                  pl.BlockSpec((SUBLANES, tb),
                               lambda bg, t: ((DN_HEADS + (bg % groups) * hb) // SUBLANES, rows(bg, t))),
                  pl.BlockSpec((1, DN_HEAD_DIM), lambda bg, t: (0, 0))],
        out_specs=head_block(0),
        out_shape=jax.ShapeDtypeStruct((m, DN_WIDTH), BF16),
        scratch_shapes=[pltpu.VMEM((hb, DN_HEAD_DIM, DN_HEAD_DIM), F32)],
        compiler_params=_params("parallel", "arbitrary"),
        name="delta_rule",
    )(qkv, qkv, qkv, proj, gb, gb_t, o_norm.reshape(1, DN_HEAD_DIM))


def _ffn(x, layer, norm, w_gate, w_up, w_down):
    xn = rmsnorm(x, norm[layer], BF16)
    h = gate_up(xn, w_gate, w_up, layer)
    return down_residual(h, w_down[layer].astype(BF16), x, 0.5)


def _token_mix(x, seq, layer, norm, w_in, conv_a, pool_w, pool_scale, conv_qkv, a_log, dt_bias, o_norm, w_out):
    xn = rmsnorm(x, norm[layer], BF16)
    n_main = 3 * A_WIDTH + POOL_WIDTH + 4 * DN_WIDTH
    w_gates = jnp.pad(w_in[layer, :, n_main:], ((0, 0), (0, LANES - 2 * DN_HEADS))).astype(BF16)
    proj = matmul_layer(xn, w_in, layer, n_main, F32, 1024, 512)
    pg = matmul(xn, w_gates, F32, 1024, LANES)
    y_ab = mix_ab(proj, conv_a[layer], pool_w[layer], pool_scale[layer], seq)
    qkv = qkv_frontend(proj, conv_qkv[layer], seq, 3 * A_WIDTH + POOL_WIDTH)
    gb = gates(pg, a_log[layer], dt_bias[layer])
    y_c = delta_rule(qkv, proj, gb, o_norm[layer], seq, 3 * A_WIDTH + POOL_WIDTH + 3 * DN_WIDTH)
    return wout_residual(y_ab, y_c, w_out, layer, x)


@jax.jit
def kernel(x, ffn1_norm, ffn1_w_gate, ffn1_w_up, ffn1_w_down, mix_norm, w_in, conv_a, pool_w, pool_scale,
           conv_qkv, a_log, dt_bias, o_norm, w_out, ffn2_norm, ffn2_w_gate, ffn2_w_up, ffn2_w_down, final_norm):
    b, t, d = x.shape
    h = x.reshape(b * t, d)
    for l in range(ffn1_norm.shape[0]):
        h = _ffn(h, l, ffn1_norm, ffn1_w_gate, ffn1_w_up, ffn1_w_down)
        h = _token_mix(h, t, l, mix_norm, w_in, conv_a, pool_w, pool_scale, conv_qkv, a_log, dt_bias, o_norm, w_out)
        h = _ffn(h, l, ffn2_norm, ffn2_w_gate, ffn2_w_up, ffn2_w_down)
    return rmsnorm(h, final_norm, F32).reshape(b, t, d)
```

```python
import functools

import jax
import jax.numpy as jnp
from jax import lax
from jax.experimental import pallas as pl
from jax.experimental.pallas import tpu as pltpu

F32 = jnp.float32
BF16 = jnp.bfloat16

NORM_EPS = 1e-6
A_WIDTH = 1024
POOL_WIDTH = 1024
POOL_WINDOWS = (2, 4, 8, 16)
POOL_GROUP_DIM = 256
DN_HEADS = 16
DN_HEAD_DIM = 128
DN_WIDTH = DN_HEADS * DN_HEAD_DIM
DN_CONV_WIDTH = 4
CHUNK = 64
HALO = 16
LANES = 128
SUBLANES = 8
VMEM_LIMIT = 56 * 1024 * 1024


def _params(*semantics):
    return pltpu.CompilerParams(dimension_semantics=semantics, vmem_limit_bytes=VMEM_LIMIT)


def _sigmoid(x):
    return 0.5 * (1.0 + jnp.tanh(0.5 * x))


def _silu(x):
    m = 0.5 * x
    return m + m * jnp.tanh(m)


def _row_sumsq(x, shape):
    return jnp.broadcast_to(jnp.sum(x * x, axis=-1, keepdims=True), shape)


def _inv_rms(ss_ref, d):
    total = jnp.sum(ss_ref[...], axis=-1, keepdims=True) * (1.0 / LANES)
    return lax.rsqrt(total / d + NORM_EPS)


def _rmsnorm_body(x_ref, g_ref, o_ref):
    x = x_ref[...]
    ms = jnp.mean(x * x, axis=-1, keepdims=True)
    o_ref[...] = (x * lax.rsqrt(ms + NORM_EPS) * g_ref[...]).astype(o_ref.dtype)


def rmsnorm(x, gain, out_dtype, tm=512):
    m, d = x.shape
    return pl.pallas_call(
        _rmsnorm_body,
        grid=(m // tm,),
        in_specs=[pl.BlockSpec((tm, d), lambda i: (i, 0)), pl.BlockSpec((1, d), lambda i: (0, 0))],
        out_specs=pl.BlockSpec((tm, d), lambda i: (i, 0)),
        out_shape=jax.ShapeDtypeStruct((m, d), out_dtype),
        compiler_params=_params("parallel"),
        name="rmsnorm",
    )(x, gain.reshape(1, d))


def _norm_prep_body(x_ref, g_ref, xg_ref, ss_ref):
    x = x_ref[...]
    xg_ref[...] = (x * g_ref[...]).astype(xg_ref.dtype)
    ss_ref[...] = _row_sumsq(x, ss_ref.shape)


def norm_prep(x, gain, tm=512):
    m, d = x.shape
    return pl.pallas_call(
        _norm_prep_body,
        grid=(m // tm,),
        in_specs=[pl.BlockSpec((tm, d), lambda i: (i, 0)), pl.BlockSpec((1, d), lambda i: (0, 0))],
        out_specs=[pl.BlockSpec((tm, d), lambda i: (i, 0)), pl.BlockSpec((tm, LANES), lambda i: (i, 0))],
        out_shape=[jax.ShapeDtypeStruct((m, d), BF16), jax.ShapeDtypeStruct((m, LANES), F32)],
        compiler_params=_params("parallel"),
        name="norm_prep",
    )(x, gain.reshape(1, d))


def _cast_body(w_ref, o_ref):
    o_ref[...] = w_ref[...].astype(o_ref.dtype)


def cast_layer(w, layer, tr=512):
    _, r, c = w.shape
    return pl.pallas_call(
        _cast_body,
        grid=(r // tr,),
        in_specs=[pl.BlockSpec((None, tr, c), lambda i: (layer, i, 0))],
        out_specs=pl.BlockSpec((tr, c), lambda i: (i, 0)),
        out_shape=jax.ShapeDtypeStruct((r, c), BF16),
        compiler_params=_params("parallel"),
        name="cast_layer",
    )(w)


def _mm_nt_body(a_ref, ss_ref, b_ref, o_ref):
    acc = lax.dot_general(a_ref[...], b_ref[...].astype(BF16), (((1,), (1,)), ((), ())),
                          preferred_element_type=F32)
    o_ref[...] = (acc * _inv_rms(ss_ref, a_ref.shape[1])).astype(o_ref.dtype)


def matmul_nt_normed(a, ss, w_t, layer, n, out_dtype, tm, tn):
    m, k = a.shape
    return pl.pallas_call(
        _mm_nt_body,
        grid=(m // tm, n // tn),
        in_specs=[pl.BlockSpec((tm, k), lambda i, j: (i, 0)),
                  pl.BlockSpec((tm, ss.shape[1]), lambda i, j: (i, 0)),
                  pl.BlockSpec((None, tn, k), lambda i, j: (layer, j, 0))],
        out_specs=pl.BlockSpec((tm, tn), lambda i, j: (i, j)),
        out_shape=jax.ShapeDtypeStruct((m, n), out_dtype),
        compiler_params=_params("parallel", "arbitrary"),
        name="matmul_nt_normed",
    )(a, ss, w_t)


def _gateup_body(a_ref, ss_ref, wg_ref, wu_ref, o_ref):
    a = a_ref[...]
    inv = _inv_rms(ss_ref, a.shape[1])
    g = jnp.dot(a, wg_ref[...].astype(BF16), preferred_element_type=F32) * inv
    u = jnp.dot(a, wu_ref[...].astype(BF16), preferred_element_type=F32) * inv
    o_ref[...] = (_silu(g) * u).astype(o_ref.dtype)


def gate_up(a, ss, wg, wu, layer, tm=1024, tn=256):
    m, k = a.shape
    n = wg.shape[2]
    return pl.pallas_call(
        _gateup_body,
        grid=(m // tm, n // tn),
        in_specs=[pl.BlockSpec((tm, k), lambda i, j: (i, 0)),
                  pl.BlockSpec((tm, ss.shape[1]), lambda i, j: (i, 0)),
                  pl.BlockSpec((None, k, tn), lambda i, j: (layer, 0, j)),
                  pl.BlockSpec((None, k, tn), lambda i, j: (layer, 0, j))],
        out_specs=pl.BlockSpec((tm, tn), lambda i, j: (i, j)),
        out_shape=jax.ShapeDtypeStruct((m, n), BF16),
        compiler_params=_params("parallel", "arbitrary"),
        name="gate_up",
    )(a, ss, wg, wu)


def _down_body(h_ref, w_ref, x_ref, g_ref, o_ref, xg_ref, ss_ref, acc_ref, *, scale):
    k = pl.program_id(2)

    @pl.when(k == 0)
    def _():
        acc_ref[...] = jnp.zeros_like(acc_ref)

    acc_ref[...] += jnp.dot(h_ref[...], w_ref[...], preferred_element_type=F32)

    @pl.when(k == pl.num_programs(2) - 1)
    def _():
        o = x_ref[...] + scale * acc_ref[...]
        o_ref[...] = o
        xg_ref[...] = (o * g_ref[...]).astype(xg_ref.dtype)
        ss_ref[...] = _row_sumsq(o, ss_ref.shape)


def down_residual(h, w, x, scale, next_gain, tm=1024, tn=1024, tk=2048):
    m, k = h.shape
    n = w.shape[1]
    nj = n // tn
    return pl.pallas_call(
        functools.partial(_down_body, scale=scale),
        grid=(m // tm, nj, k // tk),
        in_specs=[pl.BlockSpec((tm, tk), lambda i, j, kk: (i, kk)),
                  pl.BlockSpec((tk, tn), lambda i, j, kk: (kk, j)),
                  pl.BlockSpec((tm, tn), lambda i, j, kk: (i, j)),
                  pl.BlockSpec((1, tn), lambda i, j, kk: (0, j))],
        out_specs=[pl.BlockSpec((tm, tn), lambda i, j, kk: (i, j)),
                   pl.BlockSpec((tm, tn), lambda i, j, kk: (i, j)),
                   pl.BlockSpec((tm, LANES), lambda i, j, kk: (i, j))],
        out_shape=[jax.ShapeDtypeStruct((m, n), F32), jax.ShapeDtypeStruct((m, n), BF16),
                   jax.ShapeDtypeStruct((m, nj * LANES), F32)],
        scratch_shapes=[pltpu.VMEM((tm, tn), F32)],
        compiler_params=_params("parallel", "parallel", "arbitrary"),
        name="down_residual",
    )(h, w, x, next_gain.reshape(1, n))


def _wout_body(ya_ref, yc_ref, wa_ref, wc_ref, x_ref, g_ref, o_ref, xg_ref, ss_ref):
    acc = jnp.dot(ya_ref[...], wa_ref[...].astype(BF16), preferred_element_type=F32)
    acc = acc + jnp.dot(yc_ref[...], wc_ref[...].astype(BF16), preferred_element_type=F32)
    o = x_ref[...] + acc
    o_ref[...] = o
    xg_ref[...] = (o * g_ref[...]).astype(xg_ref.dtype)
    ss_ref[...] = _row_sumsq(o, ss_ref.shape)


def wout_residual(y_ab, y_c, w, layer, x, next_gain, tm=1024, tn=512):
    m, ka = y_ab.shape
    kc = y_c.shape[1]
    n = w.shape[2]
    nj = n // tn
    assert ka == kc, "the two K halves share one block shape"
    return pl.pallas_call(
        _wout_body,
        grid=(m // tm, nj),
        in_specs=[pl.BlockSpec((tm, ka), lambda i, j: (i, 0)),
                  pl.BlockSpec((tm, kc), lambda i, j: (i, 0)),
                  pl.BlockSpec((None, ka, tn), lambda i, j: (layer, 0, j)),
                  pl.BlockSpec((None, kc, tn), lambda i, j: (layer, 1, j)),
                  pl.BlockSpec((tm, tn), lambda i, j: (i, j)),
                  pl.BlockSpec((1, tn), lambda i, j: (0, j))],
        out_specs=[pl.BlockSpec((tm, tn), lambda i, j: (i, j)),
                   pl.BlockSpec((tm, tn), lambda i, j: (i, j)),
                   pl.BlockSpec((tm, LANES), lambda i, j: (i, j))],
        out_shape=[jax.ShapeDtypeStruct((m, n), F32), jax.ShapeDtypeStruct((m, n), BF16),
                   jax.ShapeDtypeStruct((m, nj * LANES), F32)],
        compiler_params=_params("parallel", "arbitrary"),
        name="wout_residual",
    )(y_ab, y_c, w, w, x, next_gain.reshape(1, n))


def _mix_ab_body(ab_ref, ac_ref, ah_ref, pu_ref, ach_ref, ahh_ref, puh_ref, cw_ref, pw_ref, ps_ref,
                 o_ref, hbuf, pbuf, *, tt, tiles_per_seq):
    i = pl.program_id(0) % tiles_per_seq
    first = i == 0
    hbuf[0:HALO, :] = jnp.where(first, 0.0, ach_ref[...] * ahh_ref[...])
    hbuf[HALO:, :] = ac_ref[...] * ah_ref[...]
    cw = cw_ref[...]
    width = cw.shape[0]
    conv = cw[width - 1:width, :] * hbuf[HALO:HALO + tt, :]
    for j in range(width - 1):
        off = HALO - (width - 1) + j
        conv = conv + cw[j:j + 1, :] * hbuf[off:off + tt, :]
    o_ref[:, 0:A_WIDTH] = (ab_ref[...] * conv).astype(o_ref.dtype)

    pbuf[0:HALO, :] = jnp.where(first, 0.0, puh_ref[...])
    pbuf[HALO:, :] = pu_ref[...]
    pos = (i * tt + lax.broadcasted_iota(jnp.int32, (tt, 1), 0) + 1).astype(F32)
    for g, w in enumerate(POOL_WINDOWS):
        lo = g * POOL_GROUP_DIM
        hi = lo + POOL_GROUP_DIM
        cur = pbuf[HALO:HALO + tt, lo:hi]
        s = cur
        for j in range(1, w):
            s = s + pbuf[HALO - j:HALO - j + tt, lo:hi]
        pooled = s / jnp.minimum(pos, float(w)) - cur
        mixed = jnp.dot(pooled.astype(BF16), pw_ref[g], preferred_element_type=F32)
        o_ref[:, A_WIDTH + lo:A_WIDTH + hi] = (mixed * ps_ref[:, lo:hi]).astype(o_ref.dtype)


def mix_ab(proj, conv_a, pool_w, pool_scale, seq, tt=512):
    m = proj.shape[0]
    cb = A_WIDTH
    hb = tt // HALO

    def main(col):
        return pl.BlockSpec((tt, cb), lambda p: (p, col))

    def halo(col):
        return pl.BlockSpec((HALO, cb), lambda p: (jnp.maximum(p * hb - 1, 0), col))

    return pl.pallas_call(
        functools.partial(_mix_ab_body, tt=tt, tiles_per_seq=seq // tt),
        grid=(m // tt,),
        in_specs=[main(0), main(1), main(2), main(3), halo(1), halo(2), halo(3),
                  pl.BlockSpec(conv_a.shape, lambda p: (0, 0)),
                  pl.BlockSpec(pool_w.shape, lambda p: (0, 0, 0)),
                  pl.BlockSpec((1, POOL_WIDTH), lambda p: (0, 0))],
        out_specs=pl.BlockSpec((tt, A_WIDTH + POOL_WIDTH), lambda p: (p, 0)),
        out_shape=jax.ShapeDtypeStruct((m, A_WIDTH + POOL_WIDTH), BF16),
        scratch_shapes=[pltpu.VMEM((tt + HALO, cb), F32), pltpu.VMEM((tt + HALO, cb), F32)],
        compiler_params=_params("parallel"),
        name="mix_ab",
    )(proj, proj, proj, proj, proj, proj, proj, conv_a, pool_w.astype(BF16), pool_scale.reshape(1, POOL_WIDTH))


def _split3(x):
    hi = x.astype(BF16)
    r = x - hi.astype(F32)
    mid = r.astype(BF16)
    lo = (r - mid.astype(F32)).astype(BF16)
    return hi, mid, lo


def _gates_body(x_ref, a_ref, dt_ref, o_ref, *, tt):
    x = x_ref[...]
    beta = _sigmoid(x)
    y = x + dt_ref[...]
    softplus = jnp.maximum(y, 0.0) + jnp.log(1.0 + jnp.exp(-jnp.abs(y)))
    g = -jnp.exp(a_ref[...]) * softplus
    row = lax.broadcasted_iota(jnp.int32, (CHUNK, CHUNK), 0)
    col = lax.broadcasted_iota(jnp.int32, (CHUNK, CHUNK), 1)
    tril = (row >= col).astype(BF16)
    lane = lax.broadcasted_iota(jnp.int32, (CHUNK, LANES), 1)
    for c in range(tt // CHUNK):
        sl = slice(c * CHUNK, (c + 1) * CHUNK)
        gc = sum(jnp.dot(tril, p, preferred_element_type=F32) for p in _split3(g[sl, :]))
        o_ref[sl, :] = jnp.where(lane < DN_HEADS, beta[sl, :], gc)


def gates(pg, a_log, dt_bias, tt=512):
    m = pg.shape[0]
    pad = jnp.zeros((1, LANES), F32)
    a_row = pad.at[0, DN_HEADS:2 * DN_HEADS].set(a_log)
    dt_row = pad.at[0, DN_HEADS:2 * DN_HEADS].set(dt_bias)
    return pl.pallas_call(
        functools.partial(_gates_body, tt=tt),
        grid=(m // tt,),
        in_specs=[pl.BlockSpec((tt, LANES), lambda p: (p, 0)),
                  pl.BlockSpec((1, LANES), lambda p: (0, 0)),
                  pl.BlockSpec((1, LANES), lambda p: (0, 0))],
        out_specs=pl.BlockSpec((tt, LANES), lambda p: (p, 0)),
        out_shape=jax.ShapeDtypeStruct((m, LANES), F32),
        compiler_params=_params("parallel"),
        name="gates",
    )(pg, a_row, dt_row)


def _dot(a, b):
    return jnp.dot(a.astype(BF16), b.astype(BF16), preferred_element_type=F32)


def _dot_nt(a, b):
    return lax.dot_general(a.astype(BF16), b.astype(BF16), (((1,), (1,)), ((), ())), preferred_element_type=F32)


def _dot_tn(a, b):
    return lax.dot_general(a.astype(BF16), b.astype(BF16), (((0,), (0,)), ((), ())), preferred_element_type=F32)


def _level_mask(row, col, s):
    return ((row // (2 * s)) == (col // (2 * s))) & ((row // s) % 2 == 1) & ((col // s) % 2 == 0)


def _delta_body(q_ref, k_ref, v_ref, z_ref, cq_ref, ck_ref, cv_ref, gb_ref, gt_ref, on_ref, o_ref,
                s_ref, tail_ref, head_ref, *, tb, hb):
    head0 = (pl.program_id(0) % (DN_HEADS // hb)) * hb
    nc = tb // CHUNK
    items = [(j, c) for j in range(hb) for c in range(nc)]
    raw_refs = (q_ref, k_ref, v_ref)
    conv_refs = (cq_ref, ck_ref, cv_ref)

    @pl.when(pl.program_id(1) == 0)
    def _():
        s_ref[...] = jnp.zeros_like(s_ref)
        tail_ref[...] = jnp.zeros_like(tail_ref)

    for a in range(3):
        head_ref[a, 0:HALO, :] = tail_ref[a]
        head_ref[a, HALO:, :] = raw_refs[a][0:CHUNK, :]

    def rows(c):
        return slice(c * CHUNK, (c + 1) * CHUNK)

    def lanes(j):
        return slice(j * DN_HEAD_DIM, (j + 1) * DN_HEAD_DIM)

    def front(a, it, normalise):
        j, c = it
        cw = conv_refs[a][:, lanes(j)]
        src, base = (head_ref.at[a], HALO) if c == 0 else (raw_refs[a], c * CHUNK)
        ext = SUBLANES
        xe = src[base - 2 * ext:base + CHUNK, lanes(j)]
        x0 = xe[ext:, :]
        x1 = pltpu.roll(xe, 1, 0)[ext:, :]
        b = cw[1:2, :] * x0 + cw[0:1, :] * x1
        acc = cw[3:4, :] * x0[ext:, :] + cw[2:3, :] * x1[ext:, :] + pltpu.roll(b, 2, 0)[ext:, :]
        y = _silu(acc)
        if normalise:
            y = y * lax.rsqrt(jnp.sum(y * y, axis=-1, keepdims=True) + NORM_EPS)
        return y

    sel_r = lax.broadcasted_iota(jnp.int32, (LANES, 2 * hb * LANES), 0)
    sel_c = lax.broadcasted_iota(jnp.int32, (LANES, 2 * hb * LANES), 1)
    src_lane = head0 + sel_c // (2 * LANES) + DN_HEADS * ((sel_c // LANES) % 2)
    select = (sel_r == src_lane).astype(BF16)
    bcast = sum(jnp.dot(p, select, preferred_element_type=F32) for p in _split3(gb_ref[...]))
    gt_row0 = (DN_HEADS + head0) % SUBLANES
    gt_rows = [gt_ref[pl.ds(gt_row0 + j, 1), :] for j in range(hb)]

    row = lax.broadcasted_iota(jnp.int32, (CHUNK, CHUNK), 0)
    col = lax.broadcasted_iota(jnp.int32, (CHUNK, CHUNK), 1)
    causal = row >= col
    strict = row > col
    eye = (row == col).astype(F32)
    scale = DN_HEAD_DIM ** -0.5
    gain = on_ref[...]

    q, k, kb, beta, gc, e_gc, g_last = {}, {}, {}, {}, {}, {}, {}
    decay, rhs = {}, {}
    for it in items:
        j, c = it
        q[it] = front(0, it, True) * scale
        k[it] = front(1, it, True)
        beta[it] = bcast[rows(c), 2 * j * LANES:(2 * j + 1) * LANES]
        gc[it] = bcast[rows(c), (2 * j + 1) * LANES:(2 * j + 2) * LANES]
        gc_row = gt_rows[j][:, c * CHUNK:(c + 1) * CHUNK]
        decay[it] = jnp.exp(jnp.where(causal, gc[it][:, 0:CHUNK] - gc_row, -jnp.inf))
        kb[it] = k[it] * beta[it]
        e_gc[it] = jnp.exp(gc[it])
        g_last[it] = gc[it][CHUNK - 1:CHUNK, :]
        rhs[it] = jnp.concatenate([front(2, it, False) * beta[it], kb[it] * e_gc[it]], axis=1)

    for a in range(3):
        tail_ref[a] = raw_refs[a][tb - HALO:tb, :]

    kq = {it: _dot_nt(jnp.concatenate([kb[it], q[it]], axis=0), k[it]) for it in items}
    attn = {it: (kq[it][CHUNK:, :] * decay[it]).astype(BF16) for it in items}

    lower = {it: jnp.where(strict, kq[it][0:CHUNK, :] * decay[it], 0.0) for it in items}
    x = {it: eye - jnp.where(_level_mask(row, col, 1), lower[it], 0.0) for it in items}
    s = 2
    while s < CHUNK:
        mask = _level_mask(row, col, s)
        xb = {it: x[it].astype(BF16) for it in items}
        y = {it: _dot(jnp.where(mask, lower[it], 0.0), xb[it]) for it in items}
        z = {it: _dot(xb[it], y[it]) for it in items}
        x = {it: x[it] - z[it] for it in items}
        s *= 2

    sol = {it: _dot(x[it], rhs[it]) for it in items}
    u = {it: sol[it][:, 0:DN_HEAD_DIM] for it in items}
    w = {it: sol[it][:, DN_HEAD_DIM:].astype(BF16) for it in items}
    k_state = {it: (k[it] * jnp.exp(g_last[it] - gc[it])).astype(BF16) for it in items}
    q_state = {it: (q[it] * e_gc[it]).astype(BF16) for it in items}

    state = [s_ref[j] for j in range(hb)]
    for c in range(nc):
        its = [(j, c) for j in range(hb)]
        sb = [state[j].astype(BF16) for j in range(hb)]
        ws = [_dot(w[it], sb[it[0]]) for it in its]
        qs = [_dot(q_state[it], sb[it[0]]) for it in its]
        v_new = [(u[it] - ws[it[0]]).astype(BF16) for it in its]
        upd = [_dot_tn(k_state[it], v_new[it[0]]) for it in its]
        intra = [_dot(attn[it], v_new[it[0]]) for it in its]
        for j in range(hb):
            state[j] = state[j] * jnp.exp(g_last[(j, c)]) + upd[j]
            o = qs[j] + intra[j]
            ms = jnp.mean(o * o, axis=-1, keepdims=True)
            zg = z_ref[rows(c), lanes(j)]
            o_ref[rows(c), lanes(j)] = (o * lax.rsqrt(ms + NORM_EPS) * gain * _silu(zg)).astype(o_ref.dtype)
    for j in range(hb):
        s_ref[j] = state[j]


def delta_rule(proj, conv_qkv, layer, gb, o_norm, seq, q_col0, tb=1024, hb=4):
    m = proj.shape[0]
    tps = seq // tb
    groups = DN_HEADS // hb
    width = hb * DN_HEAD_DIM
    cb0 = q_col0 // width
    gb_t = gb.T

    def rows(bg, t):
        return (bg // groups) * tps + t

    def head_block(part):
        return pl.BlockSpec((tb, width), lambda bg, t: (rows(bg, t), cb0 + part * groups + bg % groups))

    def conv_block(part):
        return pl.BlockSpec((None, DN_CONV_WIDTH, width), lambda bg, t: (layer, 0, part * groups + bg % groups))

    return pl.pallas_call(
        functools.partial(_delta_body, tb=tb, hb=hb),
        grid=(m // seq * groups, tps),
        in_specs=[head_block(0), head_block(1), head_block(2), head_block(3),
                  conv_block(0), conv_block(1), conv_block(2),
                  pl.BlockSpec((tb, LANES), lambda bg, t: (rows(bg, t), 0)),
                  pl.BlockSpec((SUBLANES, tb),
                               lambda bg, t: ((DN_HEADS + (bg % groups) * hb) // SUBLANES, rows(bg, t))),
                  pl.BlockSpec((1, DN_HEAD_DIM), lambda bg, t: (0, 0))],
        out_specs=pl.BlockSpec((tb, width), lambda bg, t: (rows(bg, t), bg % groups)),
        out_shape=jax.ShapeDtypeStruct((m, DN_WIDTH), BF16),
        scratch_shapes=[pltpu.VMEM((hb, DN_HEAD_DIM, DN_HEAD_DIM), F32),
                        pltpu.VMEM((3, HALO, width), F32),
                        pltpu.VMEM((3, HALO + CHUNK, width), F32)],
        compiler_params=_params("parallel", "arbitrary"),
        name="delta_rule",
    )(proj, proj, proj, proj, conv_qkv, conv_qkv, conv_qkv, gb, gb_t, o_norm.reshape(1, DN_HEAD_DIM))


def _ffn(x, xg, ss, layer, w_gate, w_up, w_down, next_gain):
    h = gate_up(xg, ss, w_gate, w_up, layer)
    return down_residual(h, cast_layer(w_down, layer), x, 0.5, next_gain)


def _token_mix(x, xg, ss, seq, layer, w_in, conv_a, pool_w, pool_scale, conv_qkv, a_log, dt_bias, o_norm, w_out,
               next_gain):
    n_main = 3 * A_WIDTH + POOL_WIDTH + 4 * DN_WIDTH
    w_in_t = jnp.swapaxes(w_in, 1, 2)
    w_gates_t = jnp.pad(w_in_t[layer, n_main:, :], ((0, LANES - 2 * DN_HEADS), (0, 0)))[None]
    proj = matmul_nt_normed(xg, ss, w_in_t, layer, n_main, F32, 1024, 512)
    pg = matmul_nt_normed(xg, ss, w_gates_t, 0, LANES, F32, 1024, LANES)
    y_ab = mix_ab(proj, conv_a[layer], pool_w[layer], pool_scale[layer], seq)
    gb = gates(pg, a_log[layer], dt_bias[layer])
    y_c = delta_rule(proj, conv_qkv, layer, gb, o_norm[layer], seq, 3 * A_WIDTH + POOL_WIDTH)
    return wout_residual(y_ab, y_c, w_out, layer, x, next_gain)


@jax.jit
def kernel(x, ffn1_norm, ffn1_w_gate, ffn1_w_up, ffn1_w_down, mix_norm, w_in, conv_a, pool_w, pool_scale,
           conv_qkv, a_log, dt_bias, o_norm, w_out, ffn2_norm, ffn2_w_gate, ffn2_w_up, ffn2_w_down, final_norm):
    b, t, d = x.shape
    depth = ffn1_norm.shape[0]
    h = x.reshape(b * t, d)
    hg, ss = norm_prep(h, ffn1_norm[0])
    for l in range(depth):
        h, hg, ss = _ffn(h, hg, ss, l, ffn1_w_gate, ffn1_w_up, ffn1_w_down, mix_norm[l])
        h, hg, ss = _token_mix(h, hg, ss, t, l, w_in, conv_a, pool_w, pool_scale, conv_qkv, a_log, dt_bias,
                               o_norm, w_out, ffn2_norm[l])
        after = ffn1_norm[l + 1] if l + 1 < depth else final_norm
        h, hg, ss = _ffn(h, hg, ss, l, ffn2_w_gate, ffn2_w_up, ffn2_w_down, after)
    return rmsnorm(h, final_norm, F32).reshape(b, t, d)
```

```python
import functools

import jax
import jax.numpy as jnp
from jax import lax
from jax.experimental import pallas as pl
from jax.experimental.pallas import tpu as pltpu

F32 = jnp.float32
BF16 = jnp.bfloat16

NORM_EPS = 1e-6
A_WIDTH = 1024
POOL_WIDTH = 1024
POOL_WINDOWS = (2, 4, 8, 16)
POOL_GROUP_DIM = 256
DN_HEADS = 16
DN_HEAD_DIM = 128
DN_WIDTH = DN_HEADS * DN_HEAD_DIM
DN_CONV_WIDTH = 4
CHUNK = 64
DELTA_GROUPS = 2
HALO = 16
LANES = 128
SUBLANES = 8
VMEM_LIMIT = 56 * 1024 * 1024


def _params(*semantics):
    return pltpu.CompilerParams(dimension_semantics=semantics, vmem_limit_bytes=VMEM_LIMIT)


def _sigmoid(x):
    return 0.5 * (1.0 + jnp.tanh(0.5 * x))


def _silu(x):
    m = 0.5 * x
    return m + m * jnp.tanh(m)


def _row_sumsq(x, shape):
    return jnp.broadcast_to(jnp.sum(x * x, axis=-1, keepdims=True), shape)


def _inv_rms(ss_ref, d):
    total = jnp.sum(ss_ref[...], axis=-1, keepdims=True) * (1.0 / LANES)
    return lax.rsqrt(total / d + NORM_EPS)


def _rmsnorm_body(x_ref, g_ref, o_ref):
    x = x_ref[...]
    ms = jnp.mean(x * x, axis=-1, keepdims=True)
    o_ref[...] = (x * lax.rsqrt(ms + NORM_EPS) * g_ref[...]).astype(o_ref.dtype)


def rmsnorm(x, gain, out_dtype, tm=512):
    m, d = x.shape
    return pl.pallas_call(
        _rmsnorm_body,
        grid=(m // tm,),
        in_specs=[pl.BlockSpec((tm, d), lambda i: (i, 0)), pl.BlockSpec((1, d), lambda i: (0, 0))],
        out_specs=pl.BlockSpec((tm, d), lambda i: (i, 0)),
        out_shape=jax.ShapeDtypeStruct((m, d), out_dtype),
        compiler_params=_params("parallel"),
        name="rmsnorm",
    )(x, gain.reshape(1, d))


def _norm_prep_body(x_ref, g_ref, xg_ref, ss_ref):
    x = x_ref[...]
    xg_ref[...] = (x * g_ref[...]).astype(xg_ref.dtype)
    ss_ref[...] = _row_sumsq(x, ss_ref.shape)


def norm_prep(x, gain, tm=512):
    m, d = x.shape
    return pl.pallas_call(
        _norm_prep_body,
        grid=(m // tm,),
        in_specs=[pl.BlockSpec((tm, d), lambda i: (i, 0)), pl.BlockSpec((1, d), lambda i: (0, 0))],
        out_specs=[pl.BlockSpec((tm, d), lambda i: (i, 0)), pl.BlockSpec((tm, LANES), lambda i: (i, 0))],
        out_shape=[jax.ShapeDtypeStruct((m, d), BF16), jax.ShapeDtypeStruct((m, LANES), F32)],
        compiler_params=_params("parallel"),
        name="norm_prep",
    )(x, gain.reshape(1, d))


def _cast_body(w_ref, o_ref):
    o_ref[...] = w_ref[...].astype(o_ref.dtype)


def cast_layer(w, layer, tr=512):
    _, r, c = w.shape
    return pl.pallas_call(
        _cast_body,
        grid=(r // tr,),
        in_specs=[pl.BlockSpec((None, tr, c), lambda i: (layer, i, 0))],
        out_specs=pl.BlockSpec((tr, c), lambda i: (i, 0)),
        out_shape=jax.ShapeDtypeStruct((r, c), BF16),
        compiler_params=_params("parallel"),
        name="cast_layer",
    )(w)


def _mm_nt_body(a_ref, ss_ref, b_ref, o_ref):
    acc = lax.dot_general(a_ref[...], b_ref[...].astype(BF16), (((1,), (1,)), ((), ())),
                          preferred_element_type=F32)
    o_ref[...] = (acc * _inv_rms(ss_ref, a_ref.shape[1])).astype(o_ref.dtype)


def matmul_nt_normed(a, ss, w_t, layer, n, out_dtype, tm, tn):
    m, k = a.shape
    return pl.pallas_call(
        _mm_nt_body,
        grid=(m // tm, n // tn),
        in_specs=[pl.BlockSpec((tm, k), lambda i, j: (i, 0), pipeline_mode=pl.Buffered(1)),
                  pl.BlockSpec((tm, ss.shape[1]), lambda i, j: (i, 0)),
                  pl.BlockSpec((None, tn, k), lambda i, j: (layer, j, 0))],
        out_specs=pl.BlockSpec((tm, tn), lambda i, j: (i, j)),
        out_shape=jax.ShapeDtypeStruct((m, n), out_dtype),
        compiler_params=_params("parallel", "arbitrary"),
        name="matmul_nt_normed",
    )(a, ss, w_t)


def _gateup_body(a_ref, ss_ref, wg_ref, wu_ref, o_ref):
    a = a_ref[...]
    inv = _inv_rms(ss_ref, a.shape[1])
    g = jnp.dot(a, wg_ref[...].astype(BF16), preferred_element_type=F32) * inv
    u = jnp.dot(a, wu_ref[...].astype(BF16), preferred_element_type=F32) * inv
    o_ref[...] = (_silu(g) * u).astype(o_ref.dtype)


def gate_up(a, ss, wg, wu, layer, tm=2048, tn=256):
    m, k = a.shape
    n = wg.shape[2]
    return pl.pallas_call(
        _gateup_body,
        grid=(m // tm, n // tn),
        in_specs=[pl.BlockSpec((tm, k), lambda i, j: (i, 0), pipeline_mode=pl.Buffered(1)),
                  pl.BlockSpec((tm, ss.shape[1]), lambda i, j: (i, 0)),
                  pl.BlockSpec((None, k, tn), lambda i, j: (layer, 0, j)),
                  pl.BlockSpec((None, k, tn), lambda i, j: (layer, 0, j))],
        out_specs=pl.BlockSpec((tm, tn), lambda i, j: (i, j)),
        out_shape=jax.ShapeDtypeStruct((m, n), BF16),
        compiler_params=_params("parallel", "arbitrary"),
        name="gate_up",
    )(a, ss, wg, wu)


def _down_body(h_ref, w_ref, x_ref, g_ref, o_ref, xg_ref, ss_ref, acc_ref, *, scale):
    k = pl.program_id(2)
    last = pl.num_programs(2) - 1

    def partial_product():
        return jnp.dot(h_ref[...], w_ref[...], preferred_element_type=F32)

    @pl.when(k == 0)
    def _():
        acc_ref[...] = partial_product()

    @pl.when((k > 0) & (k < last))
    def _():
        acc_ref[...] += partial_product()

    @pl.when(k == last)
    def _():
        o = x_ref[...] + scale * (acc_ref[...] + partial_product())
        o_ref[...] = o
        xg_ref[...] = (o * g_ref[...]).astype(xg_ref.dtype)
        ss_ref[...] = _row_sumsq(o, ss_ref.shape)


def down_residual(h, w, x, scale, next_gain, tm=1024, tn=1024, tk=2048):
    m, k = h.shape
    n = w.shape[1]
    nj = n // tn
    assert k // tk >= 2, "the first and the last K step are different branches"
    return pl.pallas_call(
        functools.partial(_down_body, scale=scale),
        grid=(m // tm, nj, k // tk),
        in_specs=[pl.BlockSpec((tm, tk), lambda i, j, kk: (i, kk)),
                  pl.BlockSpec((tk, tn), lambda i, j, kk: (kk, j)),
                  pl.BlockSpec((tm, tn), lambda i, j, kk: (i, j)),
                  pl.BlockSpec((1, tn), lambda i, j, kk: (0, j))],
        out_specs=[pl.BlockSpec((tm, tn), lambda i, j, kk: (i, j)),
                   pl.BlockSpec((tm, tn), lambda i, j, kk: (i, j)),
                   pl.BlockSpec((tm, LANES), lambda i, j, kk: (i, j))],
        out_shape=[jax.ShapeDtypeStruct((m, n), F32), jax.ShapeDtypeStruct((m, n), BF16),
                   jax.ShapeDtypeStruct((m, nj * LANES), F32)],
        scratch_shapes=[pltpu.VMEM((tm, tn), F32)],
        compiler_params=_params("parallel", "parallel", "arbitrary"),
        name="down_residual",
    )(h, w, x, next_gain.reshape(1, n))


def _wout_body(ya_ref, yc_ref, wa_ref, wc_ref, x_ref, g_ref, o_ref, xg_ref, ss_ref):
    acc = jnp.dot(ya_ref[...], wa_ref[...], preferred_element_type=F32)
    acc = acc + jnp.dot(yc_ref[...], wc_ref[...], preferred_element_type=F32)
    o = x_ref[...] + acc
    o_ref[...] = o
    xg_ref[...] = (o * g_ref[...]).astype(xg_ref.dtype)
    ss_ref[...] = _row_sumsq(o, ss_ref.shape)


def wout_residual(y_ab, y_c, w, x, next_gain, tm=1024, tn=512):
    m, ka = y_ab.shape
    kc = y_c.shape[1]
    n = w.shape[1]
    nj = n // tn
    assert ka == kc, "the two K halves share one block shape"
    return pl.pallas_call(
        _wout_body,
        grid=(m // tm, nj),
        in_specs=[pl.BlockSpec((tm, ka), lambda i, j: (i, 0)),
                  pl.BlockSpec((tm, kc), lambda i, j: (i, 0)),
                  pl.BlockSpec((ka, tn), lambda i, j: (0, j)),
                  pl.BlockSpec((kc, tn), lambda i, j: (1, j)),
                  pl.BlockSpec((tm, tn), lambda i, j: (i, j)),
                  pl.BlockSpec((1, tn), lambda i, j: (0, j))],
        out_specs=[pl.BlockSpec((tm, tn), lambda i, j: (i, j)),
                   pl.BlockSpec((tm, tn), lambda i, j: (i, j)),
                   pl.BlockSpec((tm, LANES), lambda i, j: (i, j))],
        out_shape=[jax.ShapeDtypeStruct((m, n), F32), jax.ShapeDtypeStruct((m, n), BF16),
                   jax.ShapeDtypeStruct((m, nj * LANES), F32)],
        compiler_params=_params("parallel", "arbitrary"),
        name="wout_residual",
    )(y_ab, y_c, w, w, x, next_gain.reshape(1, n))


def _mix_ab_body(ab_ref, ac_ref, ah_ref, pu_ref, ach_ref, ahh_ref, puh_ref, cw_ref, pw_ref, ps_ref,
                 o_ref, hbuf, pbuf, *, tt, tiles_per_seq):
    i = pl.program_id(0) % tiles_per_seq
    first = i == 0
    hbuf[0:HALO, :] = jnp.where(first, 0.0, ach_ref[...] * ahh_ref[...])
    hbuf[HALO:, :] = ac_ref[...] * ah_ref[...]
    cw = cw_ref[...]
    width = cw.shape[0]
    conv = cw[width - 1:width, :] * hbuf[HALO:HALO + tt, :]
    for j in range(width - 1):
        off = HALO - (width - 1) + j
        conv = conv + cw[j:j + 1, :] * hbuf[off:off + tt, :]
    o_ref[:, 0:A_WIDTH] = (ab_ref[...] * conv).astype(o_ref.dtype)

    pbuf[0:HALO, :] = jnp.where(first, 0.0, puh_ref[...])
    pbuf[HALO:, :] = pu_ref[...]
    pos = (i * tt + lax.broadcasted_iota(jnp.int32, (tt, 1), 0) + 1).astype(F32)
    for g, w in enumerate(POOL_WINDOWS):
        lo = g * POOL_GROUP_DIM
        hi = lo + POOL_GROUP_DIM
        cur = pbuf[HALO:HALO + tt, lo:hi]
        s = cur
        for j in range(1, w):
            s = s + pbuf[HALO - j:HALO - j + tt, lo:hi]
        pooled = s / jnp.minimum(pos, float(w)) - cur
        mixed = jnp.dot(pooled.astype(BF16), pw_ref[g], preferred_element_type=F32)
        o_ref[:, A_WIDTH + lo:A_WIDTH + hi] = (mixed * ps_ref[:, lo:hi]).astype(o_ref.dtype)


def mix_ab(proj, conv_a, pool_w, pool_scale, seq, tt=512):
    m = proj.shape[0]
    cb = A_WIDTH
    hb = tt // HALO

    def main(col):
        return pl.BlockSpec((tt, cb), lambda p: (p, col))

    def halo(col):
        return pl.BlockSpec((HALO, cb), lambda p: (jnp.maximum(p * hb - 1, 0), col))

    return pl.pallas_call(
        functools.partial(_mix_ab_body, tt=tt, tiles_per_seq=seq // tt),
        grid=(m // tt,),
        in_specs=[main(0), main(1), main(2), main(3), halo(1), halo(2), halo(3),
                  pl.BlockSpec(conv_a.shape, lambda p: (0, 0)),
                  pl.BlockSpec(pool_w.shape, lambda p: (0, 0, 0)),
                  pl.BlockSpec((1, POOL_WIDTH), lambda p: (0, 0))],
        out_specs=pl.BlockSpec((tt, A_WIDTH + POOL_WIDTH), lambda p: (p, 0)),
        out_shape=jax.ShapeDtypeStruct((m, A_WIDTH + POOL_WIDTH), BF16),
        scratch_shapes=[pltpu.VMEM((tt + HALO, cb), F32), pltpu.VMEM((tt + HALO, cb), F32)],
        compiler_params=_params("parallel"),
        name="mix_ab",
    )(proj, proj, proj, proj, proj, proj, proj, conv_a, pool_w.astype(BF16), pool_scale.reshape(1, POOL_WIDTH))


def _split3(x):
    hi = x.astype(BF16)
    r = x - hi.astype(F32)
    mid = r.astype(BF16)
    lo = (r - mid.astype(F32)).astype(BF16)
    return hi, mid, lo


def _gates_body(x_ref, a_ref, dt_ref, o_ref, *, tt):
    x = x_ref[...]
    beta = _sigmoid(x)
    y = x + dt_ref[...]
    softplus = jnp.maximum(y, 0.0) + jnp.log(1.0 + jnp.exp(-jnp.abs(y)))
    g = -jnp.exp(a_ref[...]) * softplus
    row = lax.broadcasted_iota(jnp.int32, (CHUNK, CHUNK), 0)
    col = lax.broadcasted_iota(jnp.int32, (CHUNK, CHUNK), 1)
    tril = (row >= col).astype(BF16)
    lane = lax.broadcasted_iota(jnp.int32, (CHUNK, LANES), 1)
    for c in range(tt // CHUNK):
        sl = slice(c * CHUNK, (c + 1) * CHUNK)
        gc = sum(jnp.dot(tril, p, preferred_element_type=F32) for p in _split3(g[sl, :]))
        o_ref[sl, :] = jnp.where(lane < DN_HEADS, beta[sl, :], gc)


def gates(pg, a_log, dt_bias, tt=512):
    m = pg.shape[0]
    pad = jnp.zeros((1, LANES), F32)
    a_row = pad.at[0, DN_HEADS:2 * DN_HEADS].set(a_log)
    dt_row = pad.at[0, DN_HEADS:2 * DN_HEADS].set(dt_bias)
    return pl.pallas_call(
        functools.partial(_gates_body, tt=tt),
        grid=(m // tt,),
        in_specs=[pl.BlockSpec((tt, LANES), lambda p: (p, 0)),
                  pl.BlockSpec((1, LANES), lambda p: (0, 0)),
                  pl.BlockSpec((1, LANES), lambda p: (0, 0))],
        out_specs=pl.BlockSpec((tt, LANES), lambda p: (p, 0)),
        out_shape=jax.ShapeDtypeStruct((m, LANES), F32),
        compiler_params=_params("parallel"),
        name="gates",
    )(pg, a_row, dt_row)


def _dot(a, b):
    return jnp.dot(a.astype(BF16), b.astype(BF16), preferred_element_type=F32)


def _dot_nt(a, b):
    return lax.dot_general(a.astype(BF16), b.astype(BF16), (((1,), (1,)), ((), ())), preferred_element_type=F32)


def _dot_tn(a, b):
    return lax.dot_general(a.astype(BF16), b.astype(BF16), (((0,), (0,)), ((), ())), preferred_element_type=F32)


def _level_mask(row, col, s):
    return ((row // (2 * s)) == (col // (2 * s))) & ((row // s) % 2 == 1) & ((col // s) % 2 == 0)


def _delta_body(q_ref, k_ref, v_ref, z_ref, cq_ref, ck_ref, cv_ref, gb_ref, gt_ref, on_ref, o_ref,
                s_ref, tail_ref, head_ref, *, tb, hb):
    head0 = (pl.program_id(0) % (DN_HEADS // hb)) * hb
    nc = tb // CHUNK
    raw_refs = (q_ref, k_ref, v_ref)
    conv_refs = (cq_ref, ck_ref, cv_ref)

    @pl.when(pl.program_id(1) == 0)
    def _():
        s_ref[...] = jnp.zeros_like(s_ref)
        tail_ref[...] = jnp.zeros_like(tail_ref)

    for a in range(3):
        head_ref[a, 0:HALO, :] = tail_ref[a]
        head_ref[a, HALO:, :] = raw_refs[a][0:CHUNK, :]

    def rows(c):
        return slice(c * CHUNK, (c + 1) * CHUNK)

    def lanes(j):
        return slice(j * DN_HEAD_DIM, (j + 1) * DN_HEAD_DIM)

    def front(a, it, normalise):
        j, c = it
        cw = conv_refs[a][:, lanes(j)]
        src, base = (head_ref.at[a], HALO) if c == 0 else (raw_refs[a], c * CHUNK)
        ext = SUBLANES
        xe = src[base - 2 * ext:base + CHUNK, lanes(j)]
        x0 = xe[ext:, :]
        x1 = pltpu.roll(xe, 1, 0)[ext:, :]
        b = cw[1:2, :] * x0 + cw[0:1, :] * x1
        acc = cw[3:4, :] * x0[ext:, :] + cw[2:3, :] * x1[ext:, :] + pltpu.roll(b, 2, 0)[ext:, :]
        y = _silu(acc)
        if normalise:
            y = y * lax.rsqrt(jnp.sum(y * y, axis=-1, keepdims=True) + NORM_EPS)
        return y

    sel_r = lax.broadcasted_iota(jnp.int32, (LANES, 2 * hb * LANES), 0)
    sel_c = lax.broadcasted_iota(jnp.int32, (LANES, 2 * hb * LANES), 1)
    src_lane = head0 + sel_c // (2 * LANES) + DN_HEADS * ((sel_c // LANES) % 2)
    select = (sel_r == src_lane).astype(BF16)
    bcast = sum(jnp.dot(p, select, preferred_element_type=F32) for p in _split3(gb_ref[...]))
    gt_row0 = (DN_HEADS + head0) % SUBLANES
    gt_rows = [gt_ref[pl.ds(gt_row0 + j, 1), :] for j in range(hb)]

    row = lax.broadcasted_iota(jnp.int32, (CHUNK, CHUNK), 0)
    col = lax.broadcasted_iota(jnp.int32, (CHUNK, CHUNK), 1)
    causal = row >= col
    strict = row > col
    eye = (row == col).astype(F32)
    scale = DN_HEAD_DIM ** -0.5
    gain = on_ref[...]

    done = {}

    def phase1(items):
        q, k, kb, gc, e_gc, g_last, decay, rhs = {}, {}, {}, {}, {}, {}, {}, {}
        for it in items:
            j, c = it
            q[it] = front(0, it, True) * scale
            k[it] = front(1, it, True)
            beta = bcast[rows(c), 2 * j * LANES:(2 * j + 1) * LANES]
            gc[it] = bcast[rows(c), (2 * j + 1) * LANES:(2 * j + 2) * LANES]
            gc_row = gt_rows[j][:, c * CHUNK:(c + 1) * CHUNK]
            decay[it] = jnp.exp(jnp.where(causal, gc[it][:, 0:CHUNK] - gc_row, -jnp.inf))
            kb[it] = k[it] * beta
            e_gc[it] = jnp.exp(gc[it])
            g_last[it] = gc[it][CHUNK - 1:CHUNK, :]
            rhs[it] = jnp.concatenate([front(2, it, False) * beta, kb[it] * e_gc[it]], axis=1)
        yield
        kq = {it: _dot_nt(jnp.concatenate([kb[it], q[it]], axis=0), k[it]) for it in items}
        yield
        attn = {it: (kq[it][CHUNK:, :] * decay[it]).astype(BF16) for it in items}
        lower = {it: jnp.where(strict, kq[it][0:CHUNK, :] * decay[it], 0.0) for it in items}
        x = {it: eye - jnp.where(_level_mask(row, col, 1), lower[it], 0.0) for it in items}
        s = 2
        while s < CHUNK:
            mask = _level_mask(row, col, s)
            xb = {it: x[it].astype(BF16) for it in items}
            y = {it: _dot(jnp.where(mask, lower[it], 0.0), xb[it]) for it in items}
            yield
            z = {it: _dot(xb[it], y[it]) for it in items}
            yield
            x = {it: x[it] - z[it] for it in items}
            s *= 2
        sol = {it: _dot(x[it], rhs[it]) for it in items}
        yield
        for it in items:
            done[it] = dict(u=sol[it][:, 0:DN_HEAD_DIM], w=sol[it][:, DN_HEAD_DIM:].astype(BF16), attn=attn[it],
                            k_state=(k[it] * jnp.exp(g_last[it] - gc[it])).astype(BF16),
                            q_state=(q[it] * e_gc[it]).astype(BF16), decay_last=jnp.exp(g_last[it]))

    state = [s_ref[j] for j in range(hb)]

    def phase2(chunks):
        for c in chunks:
            d = [done[(j, c)] for j in range(hb)]
            sb = [state[j].astype(BF16) for j in range(hb)]
            ws = [_dot(d[j]["w"], sb[j]) for j in range(hb)]
            qs = [_dot(d[j]["q_state"], sb[j]) for j in range(hb)]
            yield
            v_new = [(d[j]["u"] - ws[j]).astype(BF16) for j in range(hb)]
            upd = [_dot_tn(d[j]["k_state"], v_new[j]) for j in range(hb)]
            intra = [_dot(d[j]["attn"], v_new[j]) for j in range(hb)]
            yield
            for j in range(hb):
                state[j] = state[j] * d[j]["decay_last"] + upd[j]
                o = qs[j] + intra[j]
                ms = jnp.mean(o * o, axis=-1, keepdims=True)
                zg = z_ref[rows(c), lanes(j)]
                o_ref[rows(c), lanes(j)] = (o * lax.rsqrt(ms + NORM_EPS) * gain * _silu(zg)).astype(o_ref.dtype)

    def run(*generators):
        live = list(generators)
        while live:
            for g in list(live):
                if next(g, done) is done:
                    live.remove(g)

    per = nc // DELTA_GROUPS
    groups = [range(g * per, (g + 1) * per) for g in range(DELTA_GROUPS)]
    run(phase1([(j, c) for j in range(hb) for c in groups[0]]))
    for g in range(1, DELTA_GROUPS):
        run(phase1([(j, c) for j in range(hb) for c in groups[g]]), phase2(groups[g - 1]))
    run(phase2(groups[-1]))

    for a in range(3):
        tail_ref[a] = raw_refs[a][tb - HALO:tb, :]
    for j in range(hb):
        s_ref[j] = state[j]


def delta_rule(proj, conv_qkv, layer, gb, o_norm, seq, q_col0, tb=1024, hb=4):
    m = proj.shape[0]
    tps = seq // tb
    groups = DN_HEADS // hb
    width = hb * DN_HEAD_DIM
    cb0 = q_col0 // width
    gb_t = gb.T

    def rows(bg, t):
        return (bg // groups) * tps + t

    def head_block(part):
        return pl.BlockSpec((tb, width), lambda bg, t: (rows(bg, t), cb0 + part * groups + bg % groups))

    def conv_block(part):
        return pl.BlockSpec((None, DN_CONV_WIDTH, width), lambda bg, t: (layer, 0, part * groups + bg % groups))

    return pl.pallas_call(
        functools.partial(_delta_body, tb=tb, hb=hb),
        grid=(m // seq * groups, tps),
        in_specs=[head_block(0), head_block(1), head_block(2), head_block(3),
                  conv_block(0), conv_block(1), conv_block(2),
                  pl.BlockSpec((tb, LANES), lambda bg, t: (rows(bg, t), 0)),
                  pl.BlockSpec((SUBLANES, tb),
                               lambda bg, t: ((DN_HEADS + (bg % groups) * hb) // SUBLANES, rows(bg, t))),
                  pl.BlockSpec((1, DN_HEAD_DIM), lambda bg, t: (0, 0))],
        out_specs=pl.BlockSpec((tb, width), lambda bg, t: (rows(bg, t), bg % groups)),
        out_shape=jax.ShapeDtypeStruct((m, DN_WIDTH), BF16),
        scratch_shapes=[pltpu.VMEM((hb, DN_HEAD_DIM, DN_HEAD_DIM), F32),
                        pltpu.VMEM((3, HALO, width), F32),
                        pltpu.VMEM((3, HALO + CHUNK, width), F32)],
        compiler_params=_params("parallel", "arbitrary"),
        name="delta_rule",
    )(proj, proj, proj, proj, conv_qkv, conv_qkv, conv_qkv, gb, gb_t, o_norm.reshape(1, DN_HEAD_DIM))


def _ffn(x, xg, ss, layer, w_gate, w_up, w_down, next_gain):
    h = gate_up(xg, ss, w_gate, w_up, layer)
    return down_residual(h, cast_layer(w_down, layer), x, 0.5, next_gain)


def _token_mix(x, xg, ss, seq, layer, w_in, conv_a, pool_w, pool_scale, conv_qkv, a_log, dt_bias, o_norm, w_out,
               next_gain):
    n_main = 3 * A_WIDTH + POOL_WIDTH + 4 * DN_WIDTH
    w_in_t = jnp.swapaxes(w_in, 1, 2)
    w_gates_t = jnp.pad(w_in_t[layer, n_main:, :], ((0, LANES - 2 * DN_HEADS), (0, 0)))[None]
    proj = matmul_nt_normed(xg, ss, w_in_t, layer, n_main, F32, 2048, 512)
    pg = matmul_nt_normed(xg, ss, w_gates_t, 0, LANES, F32, 2048, LANES)
    y_ab = mix_ab(proj, conv_a[layer], pool_w[layer], pool_scale[layer], seq)
    gb = gates(pg, a_log[layer], dt_bias[layer])
    y_c = delta_rule(proj, conv_qkv, layer, gb, o_norm[layer], seq, 3 * A_WIDTH + POOL_WIDTH)
    return wout_residual(y_ab, y_c, cast_layer(w_out, layer), x, next_gain)


@jax.jit
def kernel(x, ffn1_norm, ffn1_w_gate, ffn1_w_up, ffn1_w_down, mix_norm, w_in, conv_a, pool_w, pool_scale,
           conv_qkv, a_log, dt_bias, o_norm, w_out, ffn2_norm, ffn2_w_gate, ffn2_w_up, ffn2_w_down, final_norm):
    b, t, d = x.shape
    depth = ffn1_norm.shape[0]
    h = x.reshape(b * t, d)
    hg, ss = norm_prep(h, ffn1_norm[0])
    for l in range(depth):
        h, hg, ss = _ffn(h, hg, ss, l, ffn1_w_gate, ffn1_w_up, ffn1_w_down, mix_norm[l])
        h, hg, ss = _token_mix(h, hg, ss, t, l, w_in, conv_a, pool_w, pool_scale, conv_qkv, a_log, dt_bias,
                               o_norm, w_out, ffn2_norm[l])
        after = ffn1_norm[l + 1] if l + 1 < depth else final_norm
        h, hg, ss = _ffn(h, hg, ss, l, ffn2_w_gate, ffn2_w_up, ffn2_w_down, after)
    return rmsnorm(h, final_norm, F32).reshape(b, t, d)
```

```python
import functools

import jax
import jax.numpy as jnp
from jax import lax
from jax.experimental import pallas as pl
from jax.experimental.pallas import tpu as pltpu

F32 = jnp.float32
BF16 = jnp.bfloat16

NORM_EPS = 1e-6
A_WIDTH = 1024
POOL_WIDTH = 1024
POOL_WINDOWS = (2, 4, 8, 16)
POOL_GROUP_DIM = 256
DN_HEADS = 16
DN_HEAD_DIM = 128
DN_WIDTH = DN_HEADS * DN_HEAD_DIM
DN_CONV_WIDTH = 4
CHUNK = 64
ROW_CHUNKS = 4
DELTA_GROUPS = 2
HALO = 16
LANES = 128
SUBLANES = 8
VMEM_LIMIT = 56 * 1024 * 1024


def _params(*semantics):
    return pltpu.CompilerParams(dimension_semantics=semantics, vmem_limit_bytes=VMEM_LIMIT)


def _sigmoid(x):
    return 0.5 * (1.0 + jnp.tanh(0.5 * x))


def _silu(x):
    m = 0.5 * x
    return m + m * jnp.tanh(m)


def _row_sumsq(x, shape):
    return jnp.broadcast_to(jnp.sum(x * x, axis=-1, keepdims=True), shape)


def _inv_rms(ss_ref, d):
    total = jnp.sum(ss_ref[...], axis=-1, keepdims=True) * (1.0 / LANES)
    return lax.rsqrt(total / d + NORM_EPS)


def _rmsnorm_body(x_ref, g_ref, o_ref):
    x = x_ref[...]
    ms = jnp.mean(x * x, axis=-1, keepdims=True)
    o_ref[...] = (x * lax.rsqrt(ms + NORM_EPS) * g_ref[...]).astype(o_ref.dtype)


def rmsnorm(x, gain, out_dtype, tm=512):
    m, d = x.shape
    return pl.pallas_call(
        _rmsnorm_body,
        grid=(m // tm,),
        in_specs=[pl.BlockSpec((tm, d), lambda i: (i, 0)), pl.BlockSpec((1, d), lambda i: (0, 0))],
        out_specs=pl.BlockSpec((tm, d), lambda i: (i, 0)),
        out_shape=jax.ShapeDtypeStruct((m, d), out_dtype),
        compiler_params=_params("parallel"),
        name="rmsnorm",
    )(x, gain.reshape(1, d))


def _norm_prep_body(x_ref, g_ref, xg_ref, ss_ref):
    x = x_ref[...]
    xg_ref[...] = (x * g_ref[...]).astype(xg_ref.dtype)
    ss_ref[...] = _row_sumsq(x, ss_ref.shape)


def norm_prep(x, gain, tm=512):
    m, d = x.shape
    return pl.pallas_call(
        _norm_prep_body,
        grid=(m // tm,),
        in_specs=[pl.BlockSpec((tm, d), lambda i: (i, 0)), pl.BlockSpec((1, d), lambda i: (0, 0))],
        out_specs=[pl.BlockSpec((tm, d), lambda i: (i, 0)), pl.BlockSpec((tm, LANES), lambda i: (i, 0))],
        out_shape=[jax.ShapeDtypeStruct((m, d), BF16), jax.ShapeDtypeStruct((m, LANES), F32)],
        compiler_params=_params("parallel"),
        name="norm_prep",
    )(x, gain.reshape(1, d))


def _cast_body(w_ref, o_ref):
    o_ref[...] = w_ref[...].astype(o_ref.dtype)


def cast_layer(w, layer, tr=512):
    _, r, c = w.shape
    return pl.pallas_call(
        _cast_body,
        grid=(r // tr,),
        in_specs=[pl.BlockSpec((None, tr, c), lambda i: (layer, i, 0))],
        out_specs=pl.BlockSpec((tr, c), lambda i: (i, 0)),
        out_shape=jax.ShapeDtypeStruct((r, c), BF16),
        compiler_params=_params("parallel"),
        name="cast_layer",
    )(w)


def _mm_nt_body(a_ref, ss_ref, b_ref, o_ref):
    acc = lax.dot_general(a_ref[...], b_ref[...].astype(BF16), (((1,), (1,)), ((), ())),
                          preferred_element_type=F32)
    o_ref[...] = (acc * _inv_rms(ss_ref, a_ref.shape[1])).astype(o_ref.dtype)


def matmul_nt_normed(a, ss, w_t, layer, n, out_dtype, tm, tn, a_buffers):
    m, k = a.shape
    return pl.pallas_call(
        _mm_nt_body,
        grid=(m // tm, n // tn),
        in_specs=[pl.BlockSpec((tm, k), lambda i, j: (i, 0), pipeline_mode=pl.Buffered(a_buffers)),
                  pl.BlockSpec((tm, ss.shape[1]), lambda i, j: (i, 0)),
                  pl.BlockSpec((None, tn, k), lambda i, j: (layer, j, 0))],
        out_specs=pl.BlockSpec((tm, tn), lambda i, j: (i, j)),
        out_shape=jax.ShapeDtypeStruct((m, n), out_dtype),
        compiler_params=_params("parallel", "arbitrary"),
        name="matmul_nt_normed",
    )(a, ss, w_t)


def _gateup_body(a_ref, ss_ref, wg_ref, wu_ref, o_ref):
    wg = wg_ref[...].astype(BF16)
    wu = wu_ref[...].astype(BF16)
    tm, d = a_ref.shape
    rows = tm // ROW_CHUNKS
    for r in range(ROW_CHUNKS):
        sl = slice(r * rows, (r + 1) * rows)
        a = a_ref[sl, :]
        inv = lax.rsqrt(jnp.sum(ss_ref[sl, :], axis=-1, keepdims=True) * (1.0 / (LANES * d)) + NORM_EPS)
        g = jnp.dot(a, wg, preferred_element_type=F32) * inv
        u = jnp.dot(a, wu, preferred_element_type=F32) * inv
        o_ref[sl, :] = (_silu(g) * u).astype(o_ref.dtype)


def gate_up(a, ss, wg, wu, layer, tm=2048, tn=256):
    m, k = a.shape
    n = wg.shape[2]
    return pl.pallas_call(
        _gateup_body,
        grid=(m // tm, n // tn),
        in_specs=[pl.BlockSpec((tm, k), lambda i, j: (i, 0), pipeline_mode=pl.Buffered(1)),
                  pl.BlockSpec((tm, ss.shape[1]), lambda i, j: (i, 0)),
                  pl.BlockSpec((None, k, tn), lambda i, j: (layer, 0, j)),
                  pl.BlockSpec((None, k, tn), lambda i, j: (layer, 0, j))],
        out_specs=pl.BlockSpec((tm, tn), lambda i, j: (i, j)),
        out_shape=jax.ShapeDtypeStruct((m, n), BF16),
        compiler_params=_params("parallel", "arbitrary"),
        name="gate_up",
    )(a, ss, wg, wu)


def _down_body(h_ref, w_ref, x_ref, g_ref, o_ref, xg_ref, ss_ref, *, scale):
    o = x_ref[...] + scale * jnp.dot(h_ref[...], w_ref[...], preferred_element_type=F32)
    o_ref[...] = o
    xg_ref[...] = (o * g_ref[...]).astype(xg_ref.dtype)
    ss_ref[...] = _row_sumsq(o, ss_ref.shape)


def down_residual(h, w, x, scale, next_gain, tm=512, tn=1024):
    m, k = h.shape
    n = w.shape[1]
    nj = n // tn
    return pl.pallas_call(
        functools.partial(_down_body, scale=scale),
        grid=(nj, m // tm),
        in_specs=[pl.BlockSpec((tm, k), lambda j, i: (i, 0)),
                  pl.BlockSpec((k, tn), lambda j, i: (0, j), pipeline_mode=pl.Buffered(1)),
                  pl.BlockSpec((tm, tn), lambda j, i: (i, j)),
                  pl.BlockSpec((1, tn), lambda j, i: (0, j))],
        out_specs=[pl.BlockSpec((tm, tn), lambda j, i: (i, j)),
                   pl.BlockSpec((tm, tn), lambda j, i: (i, j)),
                   pl.BlockSpec((tm, LANES), lambda j, i: (i, j))],
        out_shape=[jax.ShapeDtypeStruct((m, n), F32), jax.ShapeDtypeStruct((m, n), BF16),
                   jax.ShapeDtypeStruct((m, nj * LANES), F32)],
        compiler_params=_params("parallel", "arbitrary"),
        name="down_residual",
    )(h, w, x, next_gain.reshape(1, n))


def _wout_body(ya_ref, yc_ref, wa_ref, wc_ref, x_ref, g_ref, o_ref, xg_ref, ss_ref):
    acc = jnp.dot(ya_ref[...], wa_ref[...], preferred_element_type=F32)
    acc = acc + jnp.dot(yc_ref[...], wc_ref[...], preferred_element_type=F32)
    o = x_ref[...] + acc
    o_ref[...] = o
    xg_ref[...] = (o * g_ref[...]).astype(xg_ref.dtype)
    ss_ref[...] = _row_sumsq(o, ss_ref.shape)


def wout_residual(y_ab, y_c, w, x, next_gain, tm=1024, tn=512):
    m, ka = y_ab.shape
    kc = y_c.shape[1]
    n = w.shape[1]
    nj = n // tn
    assert ka == kc, "the two K halves share one block shape"
    return pl.pallas_call(
        _wout_body,
        grid=(m // tm, nj),
        in_specs=[pl.BlockSpec((tm, ka), lambda i, j: (i, 0)),
                  pl.BlockSpec((tm, kc), lambda i, j: (i, 0)),
                  pl.BlockSpec((ka, tn), lambda i, j: (0, j)),
                  pl.BlockSpec((kc, tn), lambda i, j: (1, j)),
                  pl.BlockSpec((tm, tn), lambda i, j: (i, j)),
                  pl.BlockSpec((1, tn), lambda i, j: (0, j))],
        out_specs=[pl.BlockSpec((tm, tn), lambda i, j: (i, j)),
                   pl.BlockSpec((tm, tn), lambda i, j: (i, j)),
                   pl.BlockSpec((tm, LANES), lambda i, j: (i, j))],
        out_shape=[jax.ShapeDtypeStruct((m, n), F32), jax.ShapeDtypeStruct((m, n), BF16),
                   jax.ShapeDtypeStruct((m, nj * LANES), F32)],
        compiler_params=_params("parallel", "arbitrary"),
        name="wout_residual",
    )(y_ab, y_c, w, w, x, next_gain.reshape(1, n))


def _mix_ab_body(ab_ref, ac_ref, ah_ref, pu_ref, ach_ref, ahh_ref, puh_ref, cw_ref, pw_ref, ps_ref,
                 o_ref, hbuf, pbuf, *, tt, tiles_per_seq):
    i = pl.program_id(0) % tiles_per_seq
    first = i == 0
    hbuf[0:HALO, :] = jnp.where(first, 0.0, ach_ref[...] * ahh_ref[...])
    hbuf[HALO:, :] = ac_ref[...] * ah_ref[...]
    cw = cw_ref[...]
    width = cw.shape[0]
    conv = cw[width - 1:width, :] * hbuf[HALO:HALO + tt, :]
    for j in range(width - 1):
        off = HALO - (width - 1) + j
        conv = conv + cw[j:j + 1, :] * hbuf[off:off + tt, :]
    o_ref[:, 0:A_WIDTH] = (ab_ref[...] * conv).astype(o_ref.dtype)

    pbuf[0:HALO, :] = jnp.where(first, 0.0, puh_ref[...])
    pbuf[HALO:, :] = pu_ref[...]
    pos = (i * tt + lax.broadcasted_iota(jnp.int32, (tt, 1), 0) + 1).astype(F32)
    for g, w in enumerate(POOL_WINDOWS):
        lo = g * POOL_GROUP_DIM
        hi = lo + POOL_GROUP_DIM
        cur = pbuf[HALO:HALO + tt, lo:hi]
        s = cur
        for j in range(1, w):
            s = s + pbuf[HALO - j:HALO - j + tt, lo:hi]
        pooled = s / jnp.minimum(pos, float(w)) - cur
        mixed = jnp.dot(pooled.astype(BF16), pw_ref[g], preferred_element_type=F32)
        o_ref[:, A_WIDTH + lo:A_WIDTH + hi] = (mixed * ps_ref[:, lo:hi]).astype(o_ref.dtype)


def mix_ab(proj, conv_a, pool_w, pool_scale, seq, tt=512):
    m = proj.shape[0]
    cb = A_WIDTH
    hb = tt // HALO

    def main(col):
        return pl.BlockSpec((tt, cb), lambda p: (p, col))

    def halo(col):
        return pl.BlockSpec((HALO, cb), lambda p: (jnp.maximum(p * hb - 1, 0), col))

    return pl.pallas_call(
        functools.partial(_mix_ab_body, tt=tt, tiles_per_seq=seq // tt),
        grid=(m // tt,),
        in_specs=[main(0), main(1), main(2), main(3), halo(1), halo(2), halo(3),
                  pl.BlockSpec(conv_a.shape, lambda p: (0, 0)),
                  pl.BlockSpec(pool_w.shape, lambda p: (0, 0, 0)),
                  pl.BlockSpec((1, POOL_WIDTH), lambda p: (0, 0))],
        out_specs=pl.BlockSpec((tt, A_WIDTH + POOL_WIDTH), lambda p: (p, 0)),
        out_shape=jax.ShapeDtypeStruct((m, A_WIDTH + POOL_WIDTH), BF16),
        scratch_shapes=[pltpu.VMEM((tt + HALO, cb), F32), pltpu.VMEM((tt + HALO, cb), F32)],
        compiler_params=_params("parallel"),
        name="mix_ab",
    )(proj, proj, proj, proj, proj, proj, proj, conv_a, pool_w.astype(BF16), pool_scale.reshape(1, POOL_WIDTH))


def _split3(x):
    hi = x.astype(BF16)
    r = x - hi.astype(F32)
    mid = r.astype(BF16)
    lo = (r - mid.astype(F32)).astype(BF16)
    return hi, mid, lo


def _gates_body(x_ref, a_ref, dt_ref, o_ref, *, tt):
    x = x_ref[...]
    beta = _sigmoid(x)
    y = x + dt_ref[...]
    softplus = jnp.maximum(y, 0.0) + jnp.log(1.0 + jnp.exp(-jnp.abs(y)))
    g = -jnp.exp(a_ref[...]) * softplus
    row = lax.broadcasted_iota(jnp.int32, (CHUNK, CHUNK), 0)
    col = lax.broadcasted_iota(jnp.int32, (CHUNK, CHUNK), 1)
    tril = (row >= col).astype(BF16)
    lane = lax.broadcasted_iota(jnp.int32, (CHUNK, LANES), 1)
    for c in range(tt // CHUNK):
        sl = slice(c * CHUNK, (c + 1) * CHUNK)
        gc = sum(jnp.dot(tril, p, preferred_element_type=F32) for p in _split3(g[sl, :]))
        o_ref[sl, :] = jnp.where(lane < DN_HEADS, beta[sl, :], gc)


def gates(pg, a_log, dt_bias, tt=512):
    m = pg.shape[0]
    pad = jnp.zeros((1, LANES), F32)
    a_row = pad.at[0, DN_HEADS:2 * DN_HEADS].set(a_log)
    dt_row = pad.at[0, DN_HEADS:2 * DN_HEADS].set(dt_bias)
    return pl.pallas_call(
        functools.partial(_gates_body, tt=tt),
        grid=(m // tt,),
        in_specs=[pl.BlockSpec((tt, LANES), lambda p: (p, 0)),
                  pl.BlockSpec((1, LANES), lambda p: (0, 0)),
                  pl.BlockSpec((1, LANES), lambda p: (0, 0))],
        out_specs=pl.BlockSpec((tt, LANES), lambda p: (p, 0)),
        out_shape=jax.ShapeDtypeStruct((m, LANES), F32),
        compiler_params=_params("parallel"),
        name="gates",
    )(pg, a_row, dt_row)


def _dot(a, b):
    return jnp.dot(a.astype(BF16), b.astype(BF16), preferred_element_type=F32)


def _dot_nt(a, b):
    return lax.dot_general(a.astype(BF16), b.astype(BF16), (((1,), (1,)), ((), ())), preferred_element_type=F32)


def _dot_tn(a, b):
    return lax.dot_general(a.astype(BF16), b.astype(BF16), (((0,), (0,)), ((), ())), preferred_element_type=F32)


def _level_mask(row, col, s):
    return ((row // (2 * s)) == (col // (2 * s))) & ((row // s) % 2 == 1) & ((col // s) % 2 == 0)


def _delta_body(q_ref, k_ref, v_ref, z_ref, cq_ref, ck_ref, cv_ref, gb_ref, gt_ref, on_ref, o_ref,
                s_ref, tail_ref, head_ref, *, tb, hb):
    head0 = (pl.program_id(0) % (DN_HEADS // hb)) * hb
    nc = tb // CHUNK
    raw_refs = (q_ref, k_ref, v_ref)
    conv_refs = (cq_ref, ck_ref, cv_ref)

    @pl.when(pl.program_id(1) == 0)
    def _():
        s_ref[...] = jnp.zeros_like(s_ref)
        tail_ref[...] = jnp.zeros_like(tail_ref)

    for a in range(3):
        head_ref[a, 0:HALO, :] = tail_ref[a]
        head_ref[a, HALO:, :] = raw_refs[a][0:CHUNK, :]

    def rows(c):
        return slice(c * CHUNK, (c + 1) * CHUNK)

    def lanes(j):
        return slice(j * DN_HEAD_DIM, (j + 1) * DN_HEAD_DIM)

    def front(a, it, normalise):
        j, c = it
        cw = conv_refs[a][:, lanes(j)]
        src, base = (head_ref.at[a], HALO) if c == 0 else (raw_refs[a], c * CHUNK)
        ext = SUBLANES
        xe = src[base - 2 * ext:base + CHUNK, lanes(j)]
        x0 = xe[ext:, :]
        x1 = pltpu.roll(xe, 1, 0)[ext:, :]
        b = cw[1:2, :] * x0 + cw[0:1, :] * x1
        acc = cw[3:4, :] * x0[ext:, :] + cw[2:3, :] * x1[ext:, :] + pltpu.roll(b, 2, 0)[ext:, :]
        y = _silu(acc)
        if normalise:
            y = y * lax.rsqrt(jnp.sum(y * y, axis=-1, keepdims=True) + NORM_EPS)
        return y

    sel_r = lax.broadcasted_iota(jnp.int32, (LANES, 2 * hb * LANES), 0)
    sel_c = lax.broadcasted_iota(jnp.int32, (LANES, 2 * hb * LANES), 1)
    src_lane = head0 + sel_c // (2 * LANES) + DN_HEADS * ((sel_c // LANES) % 2)
    select = (sel_r == src_lane).astype(BF16)
    bcast = sum(jnp.dot(p, select, preferred_element_type=F32) for p in _split3(gb_ref[...]))
    gt_row0 = (DN_HEADS + head0) % SUBLANES
    gt_rows = [gt_ref[pl.ds(gt_row0 + j, 1), :] for j in range(hb)]

    row = lax.broadcasted_iota(jnp.int32, (CHUNK, CHUNK), 0)
    col = lax.broadcasted_iota(jnp.int32, (CHUNK, CHUNK), 1)
    causal = row >= col
    strict = row > col
    eye = (row == col).astype(F32)
    scale = DN_HEAD_DIM ** -0.5
    gain = on_ref[...]

    done = {}

    def phase1(items):
        q, k, kb, gc, e_gc, g_last, decay, rhs = {}, {}, {}, {}, {}, {}, {}, {}
        for it in items:
            j, c = it
            q[it] = front(0, it, True) * scale
            k[it] = front(1, it, True)
            beta = bcast[rows(c), 2 * j * LANES:(2 * j + 1) * LANES]
            gc[it] = bcast[rows(c), (2 * j + 1) * LANES:(2 * j + 2) * LANES]
            gc_row = gt_rows[j][:, c * CHUNK:(c + 1) * CHUNK]
            decay[it] = jnp.exp(jnp.where(causal, gc[it][:, 0:CHUNK] - gc_row, -jnp.inf))
            kb[it] = k[it] * beta
            e_gc[it] = jnp.exp(gc[it])
            g_last[it] = gc[it][CHUNK - 1:CHUNK, :]
            rhs[it] = jnp.concatenate([front(2, it, False) * beta, kb[it] * e_gc[it]], axis=1)
        yield
        kq = {it: _dot_nt(jnp.concatenate([kb[it], q[it]], axis=0), k[it]) for it in items}
        yield
        attn = {it: (kq[it][CHUNK:, :] * decay[it]).astype(BF16) for it in items}
        lower = {it: jnp.where(strict, kq[it][0:CHUNK, :] * decay[it], 0.0) for it in items}
        x = {it: eye - jnp.where(_level_mask(row, col, 1), lower[it], 0.0) for it in items}
        s = 2
        while s < CHUNK:
            mask = _level_mask(row, col, s)
            xb = {it: x[it].astype(BF16) for it in items}
            y = {it: _dot(jnp.where(mask, lower[it], 0.0), xb[it]) for it in items}
            yield
            z = {it: _dot(xb[it], y[it]) for it in items}
            yield
            x = {it: x[it] - z[it] for it in items}
            s *= 2
        sol = {it: _dot(x[it], rhs[it]) for it in items}
        yield
        for it in items:
            done[it] = dict(u=sol[it][:, 0:DN_HEAD_DIM], w=sol[it][:, DN_HEAD_DIM:].astype(BF16), attn=attn[it],
                            k_state=(k[it] * jnp.exp(g_last[it] - gc[it])).astype(BF16),
                            q_state=(q[it] * e_gc[it]).astype(BF16), decay_last=jnp.exp(g_last[it]))

    state = [s_ref[j] for j in range(hb)]

    def phase2(chunks):
        for c in chunks:
            d = [done[(j, c)] for j in range(hb)]
            sb = [state[j].astype(BF16) for j in range(hb)]
            ws = [_dot(d[j]["w"], sb[j]) for j in range(hb)]
            qs = [_dot(d[j]["q_state"], sb[j]) for j in range(hb)]
            yield
            v_new = [(d[j]["u"] - ws[j]).astype(BF16) for j in range(hb)]
            upd = [_dot_tn(d[j]["k_state"], v_new[j]) for j in range(hb)]
            intra = [_dot(d[j]["attn"], v_new[j]) for j in range(hb)]
            yield
            for j in range(hb):
                state[j] = state[j] * d[j]["decay_last"] + upd[j]
                o = qs[j] + intra[j]
                ms = jnp.mean(o * o, axis=-1, keepdims=True)
                zg = z_ref[rows(c), lanes(j)]
                o_ref[rows(c), lanes(j)] = (o * lax.rsqrt(ms + NORM_EPS) * gain * _silu(zg)).astype(o_ref.dtype)

    def run(*generators):
        live = list(generators)
        while live:
            for g in list(live):
                if next(g, done) is done:
                    live.remove(g)

    per = nc // DELTA_GROUPS
    groups = [range(g * per, (g + 1) * per) for g in range(DELTA_GROUPS)]
    run(phase1([(j, c) for j in range(hb) for c in groups[0]]))
    for g in range(1, DELTA_GROUPS):
        run(phase1([(j, c) for j in range(hb) for c in groups[g]]), phase2(groups[g - 1]))
    run(phase2(groups[-1]))

    for a in range(3):
        tail_ref[a] = raw_refs[a][tb - HALO:tb, :]
    for j in range(hb):
        s_ref[j] = state[j]


def delta_rule(proj, conv_qkv, layer, gb, o_norm, seq, q_col0, tb=1024, hb=4):
    m = proj.shape[0]
    tps = seq // tb
    groups = DN_HEADS // hb
    width = hb * DN_HEAD_DIM
    cb0 = q_col0 // width
    gb_t = gb.T

    def rows(bg, t):
        return (bg // groups) * tps + t

    def head_block(part):
        return pl.BlockSpec((tb, width), lambda bg, t: (rows(bg, t), cb0 + part * groups + bg % groups))

    def conv_block(part):
        return pl.BlockSpec((None, DN_CONV_WIDTH, width), lambda bg, t: (layer, 0, part * groups + bg % groups))

    return pl.pallas_call(
        functools.partial(_delta_body, tb=tb, hb=hb),
        grid=(m // seq * groups, tps),
        in_specs=[head_block(0), head_block(1), head_block(2), head_block(3),
                  conv_block(0), conv_block(1), conv_block(2),
                  pl.BlockSpec((tb, LANES), lambda bg, t: (rows(bg, t), 0)),
                  pl.BlockSpec((SUBLANES, tb),
                               lambda bg, t: ((DN_HEADS + (bg % groups) * hb) // SUBLANES, rows(bg, t))),
                  pl.BlockSpec((1, DN_HEAD_DIM), lambda bg, t: (0, 0))],
        out_specs=pl.BlockSpec((tb, width), lambda bg, t: (rows(bg, t), bg % groups)),
        out_shape=jax.ShapeDtypeStruct((m, DN_WIDTH), BF16),
        scratch_shapes=[pltpu.VMEM((hb, DN_HEAD_DIM, DN_HEAD_DIM), F32),
                        pltpu.VMEM((3, HALO, width), F32),
                        pltpu.VMEM((3, HALO + CHUNK, width), F32)],
        compiler_params=_params("parallel", "arbitrary"),
        name="delta_rule",
    )(proj, proj, proj, proj, conv_qkv, conv_qkv, conv_qkv, gb, gb_t, o_norm.reshape(1, DN_HEAD_DIM))


def _ffn(x, xg, ss, layer, w_gate, w_up, w_down, next_gain):
    h = gate_up(xg, ss, w_gate, w_up, layer)
    return down_residual(h, cast_layer(w_down, layer), x, 0.5, next_gain)


def _token_mix(x, xg, ss, seq, layer, w_in, conv_a, pool_w, pool_scale, conv_qkv, a_log, dt_bias, o_norm, w_out,
               next_gain):
    n_main = 3 * A_WIDTH + POOL_WIDTH + 4 * DN_WIDTH
    w_in_t = jnp.swapaxes(w_in, 1, 2)
    w_gates_t = jnp.pad(w_in_t[layer, n_main:, :], ((0, LANES - 2 * DN_HEADS), (0, 0)))[None]
    proj = matmul_nt_normed(xg, ss, w_in_t, layer, n_main, F32, 2048, 512, 1)
    pg = matmul_nt_normed(xg, ss, w_gates_t, 0, LANES, F32, 1024, LANES, 2)
    y_ab = mix_ab(proj, conv_a[layer], pool_w[layer], pool_scale[layer], seq)
    gb = gates(pg, a_log[layer], dt_bias[layer])
    y_c = delta_rule(proj, conv_qkv, layer, gb, o_norm[layer], seq, 3 * A_WIDTH + POOL_WIDTH)
    return wout_residual(y_ab, y_c, cast_layer(w_out, layer), x, next_gain)


@jax.jit
def kernel(x, ffn1_norm, ffn1_w_gate, ffn1_w_up, ffn1_w_down, mix_norm, w_in, conv_a, pool_w, pool_scale,
           conv_qkv, a_log, dt_bias, o_norm, w_out, ffn2_norm, ffn2_w_gate, ffn2_w_up, ffn2_w_down, final_norm):
    b, t, d = x.shape
    depth = ffn1_norm.shape[0]
    h = x.reshape(b * t, d)
    hg, ss = norm_prep(h, ffn1_norm[0])
    for l in range(depth):
        h, hg, ss = _ffn(h, hg, ss, l, ffn1_w_gate, ffn1_w_up, ffn1_w_down, mix_norm[l])
        h, hg, ss = _token_mix(h, hg, ss, t, l, w_in, conv_a, pool_w, pool_scale, conv_qkv, a_log, dt_bias,
                               o_norm, w_out, ffn2_norm[l])
        after = ffn1_norm[l + 1] if l + 1 < depth else final_norm
        h, hg, ss = _ffn(h, hg, ss, l, ffn2_w_gate, ffn2_w_up, ffn2_w_down, after)
    return rmsnorm(h, final_norm, F32).reshape(b, t, d)
```

```python
import functools

import jax
import jax.numpy as jnp
from jax import lax
from jax.experimental import pallas as pl
from jax.experimental.pallas import tpu as pltpu

F32 = jnp.float32
BF16 = jnp.bfloat16

NORM_EPS = 1e-6
A_WIDTH = 1024
POOL_WIDTH = 1024
POOL_WINDOWS = (2, 4, 8, 16)
POOL_GROUP_DIM = 256
DN_HEADS = 16
DN_HEAD_DIM = 128
DN_WIDTH = DN_HEADS * DN_HEAD_DIM
DN_CONV_WIDTH = 4
CHUNK = 64
ROW_CHUNKS = 4
DELTA_GROUPS = 2
HALO = 16
LANES = 128
SUBLANES = 8
VMEM_LIMIT = 56 * 1024 * 1024


def _params(*semantics):
    return pltpu.CompilerParams(dimension_semantics=semantics, vmem_limit_bytes=VMEM_LIMIT)


def _sigmoid(x):
    return 0.5 * (1.0 + jnp.tanh(0.5 * x))


def _silu(x):
    m = 0.5 * x
    return m + m * jnp.tanh(m)


def _row_sumsq(x, shape):
    return jnp.broadcast_to(jnp.sum(x * x, axis=-1, keepdims=True), shape)


def _inv_rms(ss_ref, d):
    total = jnp.sum(ss_ref[...], axis=-1, keepdims=True) * (1.0 / LANES)
    return lax.rsqrt(total / d + NORM_EPS)


def _rmsnorm_body(x_ref, g_ref, o_ref):
    x = x_ref[...]
    ms = jnp.mean(x * x, axis=-1, keepdims=True)
    o_ref[...] = (x * lax.rsqrt(ms + NORM_EPS) * g_ref[...]).astype(o_ref.dtype)


def rmsnorm(x, gain, out_dtype, tm=512):
    m, d = x.shape
    return pl.pallas_call(
        _rmsnorm_body,
        grid=(m // tm,),
        in_specs=[pl.BlockSpec((tm, d), lambda i: (i, 0)), pl.BlockSpec((1, d), lambda i: (0, 0))],
        out_specs=pl.BlockSpec((tm, d), lambda i: (i, 0)),
        out_shape=jax.ShapeDtypeStruct((m, d), out_dtype),
        compiler_params=_params("parallel"),
        name="rmsnorm",
    )(x, gain.reshape(1, d))


def _norm_prep_body(x_ref, g_ref, xg_ref, ss_ref):
    x = x_ref[...]
    xg_ref[...] = (x * g_ref[...]).astype(xg_ref.dtype)
    ss_ref[...] = _row_sumsq(x, ss_ref.shape)


def norm_prep(x, gain, tm=512):
    m, d = x.shape
    return pl.pallas_call(
        _norm_prep_body,
        grid=(m // tm,),
        in_specs=[pl.BlockSpec((tm, d), lambda i: (i, 0)), pl.BlockSpec((1, d), lambda i: (0, 0))],
        out_specs=[pl.BlockSpec((tm, d), lambda i: (i, 0)), pl.BlockSpec((tm, LANES), lambda i: (i, 0))],
        out_shape=[jax.ShapeDtypeStruct((m, d), BF16), jax.ShapeDtypeStruct((m, LANES), F32)],
        compiler_params=_params("parallel"),
        name="norm_prep",
    )(x, gain.reshape(1, d))


def _cast_body(w_ref, o_ref):
    o_ref[...] = w_ref[...].astype(o_ref.dtype)


def cast_layer(w, layer, tr=512):
    _, r, c = w.shape
    return pl.pallas_call(
        _cast_body,
        grid=(r // tr,),
        in_specs=[pl.BlockSpec((None, tr, c), lambda i: (layer, i, 0))],
        out_specs=pl.BlockSpec((tr, c), lambda i: (i, 0)),
        out_shape=jax.ShapeDtypeStruct((r, c), BF16),
        compiler_params=_params("parallel"),
        name="cast_layer",
    )(w)


def _mm_nt_body(a_ref, ss_ref, b_ref, o_ref):
    acc = lax.dot_general(a_ref[...], b_ref[...].astype(BF16), (((1,), (1,)), ((), ())),
                          preferred_element_type=F32)
    o_ref[...] = (acc * _inv_rms(ss_ref, a_ref.shape[1])).astype(o_ref.dtype)


def matmul_nt_normed(a, ss, w_t, layer, n, out_dtype, tm, tn, a_buffers):
    m, k = a.shape
    return pl.pallas_call(
        _mm_nt_body,
        grid=(m // tm, n // tn),
        in_specs=[pl.BlockSpec((tm, k), lambda i, j: (i, 0), pipeline_mode=pl.Buffered(a_buffers)),
                  pl.BlockSpec((tm, ss.shape[1]), lambda i, j: (i, 0)),
                  pl.BlockSpec((None, tn, k), lambda i, j: (layer, j, 0))],
        out_specs=pl.BlockSpec((tm, tn), lambda i, j: (i, j)),
        out_shape=jax.ShapeDtypeStruct((m, n), out_dtype),
        compiler_params=_params("parallel", "arbitrary"),
        name="matmul_nt_normed",
    )(a, ss, w_t)


def _gateup_body(a_ref, ss_ref, wg_ref, wu_ref, o_ref):
    wg = wg_ref[...].astype(BF16)
    wu = wu_ref[...].astype(BF16)
    tm, d = a_ref.shape
    rows = tm // ROW_CHUNKS
    for r in range(ROW_CHUNKS):
        sl = slice(r * rows, (r + 1) * rows)
        a = a_ref[sl, :]
        inv = lax.rsqrt(jnp.sum(ss_ref[sl, :], axis=-1, keepdims=True) * (1.0 / (LANES * d)) + NORM_EPS)
        g = jnp.dot(a, wg, preferred_element_type=F32) * inv
        u = jnp.dot(a, wu, preferred_element_type=F32) * inv
        o_ref[sl, :] = (_silu(g) * u).astype(o_ref.dtype)


def gate_up(a, ss, wg, wu, layer, tm=2048, tn=256):
    m, k = a.shape
    n = wg.shape[2]
    return pl.pallas_call(
        _gateup_body,
        grid=(m // tm, n // tn),
        in_specs=[pl.BlockSpec((tm, k), lambda i, j: (i, 0), pipeline_mode=pl.Buffered(1)),
                  pl.BlockSpec((tm, ss.shape[1]), lambda i, j: (i, 0)),
                  pl.BlockSpec((None, k, tn), lambda i, j: (layer, 0, j)),
                  pl.BlockSpec((None, k, tn), lambda i, j: (layer, 0, j))],
        out_specs=pl.BlockSpec((tm, tn), lambda i, j: (i, j)),
        out_shape=jax.ShapeDtypeStruct((m, n), BF16),
        compiler_params=_params("parallel", "arbitrary"),
        name="gate_up",
    )(a, ss, wg, wu)


def _down_body(h_ref, w_ref, x_ref, g_ref, o_ref, xg_ref, ss_ref, *, scale):
    o = x_ref[...] + scale * jnp.dot(h_ref[...], w_ref[...], preferred_element_type=F32)
    o_ref[...] = o
    xg_ref[...] = (o * g_ref[...]).astype(xg_ref.dtype)
    ss_ref[...] = _row_sumsq(o, ss_ref.shape)


def down_residual(h, w, x, scale, next_gain, tm=512, tn=1024):
    m, k = h.shape
    n = w.shape[1]
    nj = n // tn
    return pl.pallas_call(
        functools.partial(_down_body, scale=scale),
        grid=(nj, m // tm),
        in_specs=[pl.BlockSpec((tm, k), lambda j, i: (i, 0)),
                  pl.BlockSpec((k, tn), lambda j, i: (0, j), pipeline_mode=pl.Buffered(1)),
                  pl.BlockSpec((tm, tn), lambda j, i: (i, j)),
                  pl.BlockSpec((1, tn), lambda j, i: (0, j))],
        out_specs=[pl.BlockSpec((tm, tn), lambda j, i: (i, j)),
                   pl.BlockSpec((tm, tn), lambda j, i: (i, j)),
                   pl.BlockSpec((tm, LANES), lambda j, i: (i, j))],
        out_shape=[jax.ShapeDtypeStruct((m, n), F32), jax.ShapeDtypeStruct((m, n), BF16),
                   jax.ShapeDtypeStruct((m, nj * LANES), F32)],
        compiler_params=_params("parallel", "arbitrary"),
        name="down_residual",
    )(h, w, x, next_gain.reshape(1, n))


def _wout_body(ya_ref, yc_ref, wa_ref, wc_ref, x_ref, g_ref, o_ref, xg_ref, ss_ref):
    acc = jnp.dot(ya_ref[...], wa_ref[...], preferred_element_type=F32)
    acc = acc + jnp.dot(yc_ref[...], wc_ref[...], preferred_element_type=F32)
    o = x_ref[...] + acc
    o_ref[...] = o
    xg_ref[...] = (o * g_ref[...]).astype(xg_ref.dtype)
    ss_ref[...] = _row_sumsq(o, ss_ref.shape)


def wout_residual(y_ab, y_c, w, x, next_gain, tm=1024, tn=512):
    m, ka = y_ab.shape
    kc = y_c.shape[1]
    n = w.shape[1]
    nj = n // tn
    assert ka == kc, "the two K halves share one block shape"
    return pl.pallas_call(
        _wout_body,
        grid=(m // tm, nj),
        in_specs=[pl.BlockSpec((tm, ka), lambda i, j: (i, 0)),
                  pl.BlockSpec((tm, kc), lambda i, j: (i, 0)),
                  pl.BlockSpec((ka, tn), lambda i, j: (0, j)),
                  pl.BlockSpec((kc, tn), lambda i, j: (1, j)),
                  pl.BlockSpec((tm, tn), lambda i, j: (i, j)),
                  pl.BlockSpec((1, tn), lambda i, j: (0, j))],
        out_specs=[pl.BlockSpec((tm, tn), lambda i, j: (i, j)),
                   pl.BlockSpec((tm, tn), lambda i, j: (i, j)),
                   pl.BlockSpec((tm, LANES), lambda i, j: (i, j))],
        out_shape=[jax.ShapeDtypeStruct((m, n), F32), jax.ShapeDtypeStruct((m, n), BF16),
                   jax.ShapeDtypeStruct((m, nj * LANES), F32)],
        compiler_params=_params("parallel", "arbitrary"),
        name="wout_residual",
    )(y_ab, y_c, w, w, x, next_gain.reshape(1, n))


def _mix_ab_body(ab_ref, ac_ref, ah_ref, pu_ref, ach_ref, ahh_ref, puh_ref, cw_ref, pw_ref, ps_ref,
                 o_ref, hbuf, pbuf, *, tt, tiles_per_seq):
    i = pl.program_id(0) % tiles_per_seq
    first = i == 0
    hbuf[0:HALO, :] = jnp.where(first, 0.0, ach_ref[...] * ahh_ref[...])
    hbuf[HALO:, :] = ac_ref[...] * ah_ref[...]
    cw = cw_ref[...]
    width = cw.shape[0]
    conv = cw[width - 1:width, :] * hbuf[HALO:HALO + tt, :]
    for j in range(width - 1):
        off = HALO - (width - 1) + j
        conv = conv + cw[j:j + 1, :] * hbuf[off:off + tt, :]
    o_ref[:, 0:A_WIDTH] = (ab_ref[...] * conv).astype(o_ref.dtype)

    pbuf[0:HALO, :] = jnp.where(first, 0.0, puh_ref[...])
    pbuf[HALO:, :] = pu_ref[...]
    pos = (i * tt + lax.broadcasted_iota(jnp.int32, (tt, 1), 0) + 1).astype(F32)
    for g, w in enumerate(POOL_WINDOWS):
        lo = g * POOL_GROUP_DIM
        hi = lo + POOL_GROUP_DIM
        cur = pbuf[HALO:HALO + tt, lo:hi]
        s = cur
        for j in range(1, w):
            s = s + pbuf[HALO - j:HALO - j + tt, lo:hi]
        pooled = s / jnp.minimum(pos, float(w)) - cur
        mixed = jnp.dot(pooled.astype(BF16), pw_ref[g], preferred_element_type=F32)
        o_ref[:, A_WIDTH + lo:A_WIDTH + hi] = (mixed * ps_ref[:, lo:hi]).astype(o_ref.dtype)


def mix_ab(proj, conv_a, pool_w, pool_scale, seq, tt=512):
    m = proj.shape[0]
    cb = A_WIDTH
    hb = tt // HALO

    def main(col):
        return pl.BlockSpec((tt, cb), lambda p: (p, col))

    def halo(col):
        return pl.BlockSpec((HALO, cb), lambda p: (jnp.maximum(p * hb - 1, 0), col))

    return pl.pallas_call(
        functools.partial(_mix_ab_body, tt=tt, tiles_per_seq=seq // tt),
        grid=(m // tt,),
        in_specs=[main(0), main(1), main(2), main(3), halo(1), halo(2), halo(3),
                  pl.BlockSpec(conv_a.shape, lambda p: (0, 0)),
                  pl.BlockSpec(pool_w.shape, lambda p: (0, 0, 0)),
                  pl.BlockSpec((1, POOL_WIDTH), lambda p: (0, 0))],
        out_specs=pl.BlockSpec((tt, A_WIDTH + POOL_WIDTH), lambda p: (p, 0)),
        out_shape=jax.ShapeDtypeStruct((m, A_WIDTH + POOL_WIDTH), BF16),
        scratch_shapes=[pltpu.VMEM((tt + HALO, cb), F32), pltpu.VMEM((tt + HALO, cb), F32)],
        compiler_params=_params("parallel"),
        name="mix_ab",
    )(proj, proj, proj, proj, proj, proj, proj, conv_a, pool_w.astype(BF16), pool_scale.reshape(1, POOL_WIDTH))


def _split3(x):
    hi = x.astype(BF16)
    r = x - hi.astype(F32)
    mid = r.astype(BF16)
    lo = (r - mid.astype(F32)).astype(BF16)
    return hi, mid, lo


def _gates_body(x_ref, a_ref, dt_ref, o_ref, *, tt):
    x = x_ref[...]
    beta = _sigmoid(x)
    y = x + dt_ref[...]
    softplus = jnp.maximum(y, 0.0) + jnp.log(1.0 + jnp.exp(-jnp.abs(y)))
    g = -jnp.exp(a_ref[...]) * softplus
    row = lax.broadcasted_iota(jnp.int32, (CHUNK, CHUNK), 0)
    col = lax.broadcasted_iota(jnp.int32, (CHUNK, CHUNK), 1)
    tril = (row >= col).astype(BF16)
    lane = lax.broadcasted_iota(jnp.int32, (CHUNK, LANES), 1)
    for c in range(tt // CHUNK):
        sl = slice(c * CHUNK, (c + 1) * CHUNK)
        gc = sum(jnp.dot(tril, p, preferred_element_type=F32) for p in _split3(g[sl, :]))
        o_ref[sl, :] = jnp.where(lane < DN_HEADS, beta[sl, :], gc)


def gates(pg, a_log, dt_bias, tt=512):
    m = pg.shape[0]
    pad = jnp.zeros((1, LANES), F32)
    a_row = pad.at[0, DN_HEADS:2 * DN_HEADS].set(a_log)
    dt_row = pad.at[0, DN_HEADS:2 * DN_HEADS].set(dt_bias)
    return pl.pallas_call(
        functools.partial(_gates_body, tt=tt),
        grid=(m // tt,),
        in_specs=[pl.BlockSpec((tt, LANES), lambda p: (p, 0)),
                  pl.BlockSpec((1, LANES), lambda p: (0, 0)),
                  pl.BlockSpec((1, LANES), lambda p: (0, 0))],
        out_specs=pl.BlockSpec((tt, LANES), lambda p: (p, 0)),
        out_shape=jax.ShapeDtypeStruct((m, LANES), F32),
        compiler_params=_params("parallel"),
        name="gates",
    )(pg, a_row, dt_row)


def _dot(a, b):
    return jnp.dot(a.astype(BF16), b.astype(BF16), preferred_element_type=F32)


def _dot_nt(a, b):
    return lax.dot_general(a.astype(BF16), b.astype(BF16), (((1,), (1,)), ((), ())), preferred_element_type=F32)


def _dot_tn(a, b):
    return lax.dot_general(a.astype(BF16), b.astype(BF16), (((0,), (0,)), ((), ())), preferred_element_type=F32)


def _level_mask(row, col, s):
    return ((row // (2 * s)) == (col // (2 * s))) & ((row // s) % 2 == 1) & ((col // s) % 2 == 0)


def _delta_body(q_ref, k_ref, v_ref, z_ref, cq_ref, ck_ref, cv_ref, gb_ref, gt_ref, on_ref, o_ref,
                s_ref, tail_ref, head_ref, *, tb, hb):
    head0 = (pl.program_id(0) % (DN_HEADS // hb)) * hb
    nc = tb // CHUNK
    raw_refs = (q_ref, k_ref, v_ref)
    conv_refs = (cq_ref, ck_ref, cv_ref)

    @pl.when(pl.program_id(1) == 0)
    def _():
        s_ref[...] = jnp.zeros_like(s_ref)
        tail_ref[...] = jnp.zeros_like(tail_ref)

    for a in range(3):
        head_ref[a, 0:HALO, :] = tail_ref[a]
        head_ref[a, HALO:, :] = raw_refs[a][0:CHUNK, :]

    def rows(c):
        return slice(c * CHUNK, (c + 1) * CHUNK)

    def lanes(j):
        return slice(j * DN_HEAD_DIM, (j + 1) * DN_HEAD_DIM)

    def front(a, it, normalise):
        j, c = it
        cw = conv_refs[a][:, lanes(j)]
        src, base = (head_ref.at[a], HALO) if c == 0 else (raw_refs[a], c * CHUNK)
        ext = SUBLANES
        xe = src[base - 2 * ext:base + CHUNK, lanes(j)]
        x0 = xe[ext:, :]
        x1 = pltpu.roll(xe, 1, 0)[ext:, :]
        b = cw[1:2, :] * x0 + cw[0:1, :] * x1
        acc = cw[3:4, :] * x0[ext:, :] + cw[2:3, :] * x1[ext:, :] + pltpu.roll(b, 2, 0)[ext:, :]
        y = _silu(acc)
        if normalise:
            y = y * lax.rsqrt(jnp.sum(y * y, axis=-1, keepdims=True) + NORM_EPS)
        return y

    sel_r = lax.broadcasted_iota(jnp.int32, (LANES, 2 * hb * LANES), 0)
    sel_c = lax.broadcasted_iota(jnp.int32, (LANES, 2 * hb * LANES), 1)
    src_lane = head0 + sel_c // (2 * LANES) + DN_HEADS * ((sel_c // LANES) % 2)
    select = (sel_r == src_lane).astype(BF16)
    bcast = sum(jnp.dot(p, select, preferred_element_type=F32) for p in _split3(gb_ref[...]))
    gt_row0 = (DN_HEADS + head0) % SUBLANES
    gt_rows = [gt_ref[pl.ds(gt_row0 + j, 1), :] for j in range(hb)]

    row = lax.broadcasted_iota(jnp.int32, (CHUNK, CHUNK), 0)
    col = lax.broadcasted_iota(jnp.int32, (CHUNK, CHUNK), 1)
    causal = row >= col
    strict = row > col
    eye = (row == col).astype(F32)
    scale = DN_HEAD_DIM ** -0.5
    gain = on_ref[...]

    done = {}

    def phase1(items):
        q, k, kb, gc, e_gc, g_last, decay, rhs = {}, {}, {}, {}, {}, {}, {}, {}
        for it in items:
            j, c = it
            q[it] = front(0, it, True) * scale
            k[it] = front(1, it, True)
            beta = bcast[rows(c), 2 * j * LANES:(2 * j + 1) * LANES]
            gc[it] = bcast[rows(c), (2 * j + 1) * LANES:(2 * j + 2) * LANES]
            gc_row = gt_rows[j][:, c * CHUNK:(c + 1) * CHUNK]
            decay[it] = jnp.exp(jnp.where(causal, gc[it][:, 0:CHUNK] - gc_row, -jnp.inf))
            kb[it] = k[it] * beta
            e_gc[it] = jnp.exp(gc[it])
            g_last[it] = gc[it][CHUNK - 1:CHUNK, :]
            rhs[it] = jnp.concatenate([front(2, it, False) * beta, kb[it] * e_gc[it]], axis=1)
        yield
        kq = {it: _dot_nt(jnp.concatenate([kb[it], q[it]], axis=0), k[it]) for it in items}
        yield
        attn = {it: (kq[it][CHUNK:, :] * decay[it]).astype(BF16) for it in items}
        lower = {it: jnp.where(strict, kq[it][0:CHUNK, :] * decay[it], 0.0) for it in items}
        x = {it: eye - jnp.where(_level_mask(row, col, 1), lower[it], 0.0) for it in items}
        s = 2
        while s < CHUNK:
            mask = _level_mask(row, col, s)
            xb = {it: x[it].astype(BF16) for it in items}
            y = {it: _dot(jnp.where(mask, lower[it], 0.0), xb[it]) for it in items}
            yield
            z = {it: _dot(xb[it], y[it]) for it in items}
            yield
            x = {it: x[it] - z[it] for it in items}
            s *= 2
        sol = {it: _dot(x[it], rhs[it]) for it in items}
        yield
        for it in items:
            done[it] = dict(u=sol[it][:, 0:DN_HEAD_DIM], w=sol[it][:, DN_HEAD_DIM:].astype(BF16), attn=attn[it],
                            k_state=(k[it] * jnp.exp(g_last[it] - gc[it])).astype(BF16),
                            q_state=(q[it] * e_gc[it]).astype(BF16), decay_last=jnp.exp(g_last[it]))

    state = [s_ref[j] for j in range(hb)]

    def phase2(chunks):
        for c in chunks:
            d = [done[(j, c)] for j in range(hb)]
            sb = [state[j].astype(BF16) for j in range(hb)]
            ws = [_dot(d[j]["w"], sb[j]) for j in range(hb)]
            qs = [_dot(d[j]["q_state"], sb[j]) for j in range(hb)]
            yield
            v_new = [(d[j]["u"] - ws[j]).astype(BF16) for j in range(hb)]
            upd = [_dot_tn(d[j]["k_state"], v_new[j]) for j in range(hb)]
            intra = [_dot(d[j]["attn"], v_new[j]) for j in range(hb)]
            yield
            for j in range(hb):
                state[j] = state[j] * d[j]["decay_last"] + upd[j]
                o = qs[j] + intra[j]
                ms = jnp.mean(o * o, axis=-1, keepdims=True)
                zg = z_ref[rows(c), lanes(j)]
                o_ref[rows(c), lanes(j)] = (o * lax.rsqrt(ms + NORM_EPS) * gain * _silu(zg)).astype(o_ref.dtype)

    def run(*generators):
        live = list(generators)
        while live:
            for g in list(live):
                if next(g, done) is done:
                    live.remove(g)

    per = nc // DELTA_GROUPS
    groups = [range(g * per, (g + 1) * per) for g in range(DELTA_GROUPS)]
    run(phase1([(j, c) for j in range(hb) for c in groups[0]]))
    for g in range(1, DELTA_GROUPS):
        run(phase1([(j, c) for j in range(hb) for c in groups[g]]), phase2(groups[g - 1]))
    run(phase2(groups[-1]))

    for a in range(3):
        tail_ref[a] = raw_refs[a][tb - HALO:tb, :]
    for j in range(hb):
        s_ref[j] = state[j]


def delta_rule(proj, conv_qkv, layer, gb, o_norm, seq, q_col0, tb=512, hb=8):
    m = proj.shape[0]
    tps = seq // tb
    groups = DN_HEADS // hb
    width = hb * DN_HEAD_DIM
    cb0 = q_col0 // width
    gb_t = gb.T

    def rows(bg, t):
        return (bg // groups) * tps + t

    def head_block(part):
        return pl.BlockSpec((tb, width), lambda bg, t: (rows(bg, t), cb0 + part * groups + bg % groups))

    def conv_block(part):
        return pl.BlockSpec((None, DN_CONV_WIDTH, width), lambda bg, t: (layer, 0, part * groups + bg % groups))

    return pl.pallas_call(
        functools.partial(_delta_body, tb=tb, hb=hb),
        grid=(m // seq * groups, tps),
        in_specs=[head_block(0), head_block(1), head_block(2), head_block(3),
                  conv_block(0), conv_block(1), conv_block(2),
                  pl.BlockSpec((tb, LANES), lambda bg, t: (rows(bg, t), 0)),
                  pl.BlockSpec((SUBLANES, tb),
                               lambda bg, t: ((DN_HEADS + (bg % groups) * hb) // SUBLANES, rows(bg, t))),
                  pl.BlockSpec((1, DN_HEAD_DIM), lambda bg, t: (0, 0))],
        out_specs=pl.BlockSpec((tb, width), lambda bg, t: (rows(bg, t), bg % groups)),
        out_shape=jax.ShapeDtypeStruct((m, DN_WIDTH), BF16),
        scratch_shapes=[pltpu.VMEM((hb, DN_HEAD_DIM, DN_HEAD_DIM), F32),
                        pltpu.VMEM((3, HALO, width), F32),
                        pltpu.VMEM((3, HALO + CHUNK, width), F32)],
        compiler_params=_params("parallel", "arbitrary"),
        name="delta_rule",
    )(proj, proj, proj, proj, conv_qkv, conv_qkv, conv_qkv, gb, gb_t, o_norm.reshape(1, DN_HEAD_DIM))


def _ffn(x, xg, ss, layer, w_gate, w_up, w_down, next_gain):
    h = gate_up(xg, ss, w_gate, w_up, layer)
    return down_residual(h, cast_layer(w_down, layer), x, 0.5, next_gain)


def _token_mix(x, xg, ss, seq, layer, w_in, conv_a, pool_w, pool_scale, conv_qkv, a_log, dt_bias, o_norm, w_out,
               next_gain):
    n_main = 3 * A_WIDTH + POOL_WIDTH + 4 * DN_WIDTH
    w_in_t = jnp.swapaxes(w_in, 1, 2)
    w_gates_t = jnp.pad(w_in_t[layer, n_main:, :], ((0, LANES - 2 * DN_HEADS), (0, 0)))[None]
    proj = matmul_nt_normed(xg, ss, w_in_t, layer, n_main, F32, 2048, 512, 1)
    pg = matmul_nt_normed(xg, ss, w_gates_t, 0, LANES, F32, 1024, LANES, 2)
    y_ab = mix_ab(proj, conv_a[layer], pool_w[layer], pool_scale[layer], seq)
    gb = gates(pg, a_log[layer], dt_bias[layer])
    y_c = delta_rule(proj, conv_qkv, layer, gb, o_norm[layer], seq, 3 * A_WIDTH + POOL_WIDTH)
    return wout_residual(y_ab, y_c, cast_layer(w_out, layer), x, next_gain)


@jax.jit
def kernel(x, ffn1_norm, ffn1_w_gate, ffn1_w_up, ffn1_w_down, mix_norm, w_in, conv_a, pool_w, pool_scale,
           conv_qkv, a_log, dt_bias, o_norm, w_out, ffn2_norm, ffn2_w_gate, ffn2_w_up, ffn2_w_down, final_norm):
    b, t, d = x.shape
    depth = ffn1_norm.shape[0]
    h = x.reshape(b * t, d)
    hg, ss = norm_prep(h, ffn1_norm[0])
    for l in range(depth):
        h, hg, ss = _ffn(h, hg, ss, l, ffn1_w_gate, ffn1_w_up, ffn1_w_down, mix_norm[l])
        h, hg, ss = _token_mix(h, hg, ss, t, l, w_in, conv_a, pool_w, pool_scale, conv_qkv, a_log, dt_bias,
                               o_norm, w_out, ffn2_norm[l])
        after = ffn1_norm[l + 1] if l + 1 < depth else final_norm
        h, hg, ss = _ffn(h, hg, ss, l, ffn2_w_gate, ffn2_w_up, ffn2_w_down, after)
    return rmsnorm(h, final_norm, F32).reshape(b, t, d)
```
